```python
import math
import jax, jax.numpy as jnp
from jax import lax
import numpy as np


D_MODEL = 1024
BATCH = 8
SEQ = 4096
DEPTH = 4

CHUNK = 64
A_HEADS = 4
A_HEAD_DIM = 64
A_WIDTH = A_HEADS * A_HEAD_DIM
SGU_BLOCK = 128
B_WIDTH = 256
CONV_WIDTH = 31
C_HEADS = 8
C_HEAD_DIM = 64
C_WIDTH = C_HEADS * C_HEAD_DIM
DECAY_LORA = 64
ICLR_LORA = 64
MIX_WIDTH = A_WIDTH + B_WIDTH + C_WIDTH
PLE_DIM = 256
C_SHIFT_COLS = 3 * C_WIDTH + DECAY_LORA + ICLR_LORA
IN_COLS = 3 * A_WIDTH + 3 * B_WIDTH + C_SHIFT_COLS + C_WIDTH
RMS_EPS = 1e-6
LN_EPS = 1e-5
GN_EPS = 64e-5
DECAY_SCALE = math.exp(-0.5)

kernel_name = "hybrid_sgu_conformer_rwkv7_stream"


def rms_norm(x, g):
    xf = x.astype(jnp.float32)
    y = xf * lax.rsqrt(jnp.mean(xf * xf, axis=-1, keepdims=True) + RMS_EPS)
    return (y * g.astype(jnp.float32)).astype(x.dtype)


def layer_norm(x, g, b):
    xf = x.astype(jnp.float32)
    mu = jnp.mean(xf, axis=-1, keepdims=True)
    var = jnp.mean(jnp.square(xf - mu), axis=-1, keepdims=True)
    y = (xf - mu) * lax.rsqrt(var + LN_EPS)
    return (y * g.astype(jnp.float32) + b.astype(jnp.float32)).astype(x.dtype)


def chunk_causal_mask(n):
    ci = jnp.arange(n)[:, None] // CHUNK
    cj = jnp.arange(n)[None, :] // CHUNK
    return cj <= ci


def spatial_gating(u, v, ln_g, ln_b, w_s, b_s):
    bsz, seq, _ = v.shape
    v = layer_norm(v, ln_g, ln_b)
    vb = v.reshape(bsz, seq // SGU_BLOCK, SGU_BLOCK, A_HEADS, A_HEAD_DIM)
    w = jnp.where(chunk_causal_mask(SGU_BLOCK)[None], w_s, jnp.zeros_like(w_s))
    mixed = jnp.einsum('hij,bnjhd->bnihd', w, vb) + b_s.T[None, None, :, :, None]
    return u * mixed.reshape(bsz, seq, A_WIDTH)


def conv_module(val, glu_gate, conv_w, conv_b, ln_g, ln_b, pw_w, pw_b):
    y = val * jax.nn.sigmoid(glu_gate)
    y = lax.conv_general_dilated(
        y, conv_w[:, None, :], window_strides=(1,), padding=[(CONV_WIDTH - 1, 0)],
        dimension_numbers=('NWC', 'WIO', 'NWC'), feature_group_count=B_WIDTH) + conv_b
    y = jax.nn.silu(layer_norm(y, ln_g, ln_b))
    return y @ pw_w + pw_b


def token_shift(z, mu):
    prev = jnp.pad(z, ((0, 0), (1, 0), (0, 0)))[:, :-1]
    return z + mu * (prev - z)


def rwkv7_scan(r, w, k, v, a, b):
    bsz = r.shape[0]

    def step(state, inp):
        r_t, w_t, k_t, v_t, a_t, b_t = inp
        sa = jnp.einsum('bhvk,bhk->bhv', state, a_t)
        state = (state * w_t[:, :, None, :] + sa[..., None] * b_t[:, :, None, :]
                 + v_t[..., None] * k_t[:, :, None, :])
        y_t = jnp.einsum('bhvk,bhk->bhv', state, r_t)
        return state, y_t

    xs = (r.transpose(1, 0, 2, 3), w.transpose(1, 0, 2, 3), k.transpose(1, 0, 2, 3),
          v.transpose(1, 0, 2, 3), a.transpose(1, 0, 2, 3), b.transpose(1, 0, 2, 3))
    init = jnp.zeros((bsz, C_HEADS, C_HEAD_DIM, C_HEAD_DIM), jnp.float32)
    _, ys = lax.scan(step, init, xs)
    return ys.transpose(1, 0, 2, 3)


def rwkv7_mix(z, mu, w0, w_up, a0, a_up, k_k, k_a, r_k, lnx_g, lnx_b):
    bsz, seq, _ = z.shape
    f32 = jnp.float32
    z = token_shift(z, mu)
    r = z[..., :C_WIDTH]
    k = z[..., C_WIDTH:2 * C_WIDTH]
    v = z[..., 2 * C_WIDTH:3 * C_WIDTH]
    wd = z[..., 3 * C_WIDTH:3 * C_WIDTH + DECAY_LORA]
    ad = z[..., 3 * C_WIDTH + DECAY_LORA:]
    log_w = -DECAY_SCALE * jax.nn.sigmoid((w0 + jnp.tanh(wd) @ w_up).astype(f32))
    a = jax.nn.sigmoid((a0 + ad @ a_up).astype(f32))

    def heads(t):
        return t.astype(f32).reshape(bsz, seq, C_HEADS, C_HEAD_DIM)

    def hvec(t):
        return t.astype(f32).reshape(C_HEADS, C_HEAD_DIM)

    r, k, v, w, a = heads(r), heads(k), heads(v), jnp.exp(heads(log_w)), heads(a)
    kk = k * hvec(k_k)
    kk = kk * lax.rsqrt(jnp.maximum(jnp.sum(kk * kk, axis=-1, keepdims=True), 1e-12))
    k = k * (1.0 + (a - 1.0) * hvec(k_a))
    y = rwkv7_scan(r, w, k, v, -kk, kk * a)
    mu_y = jnp.mean(y, axis=-1, keepdims=True)
    var_y = jnp.mean(jnp.square(y - mu_y), axis=-1, keepdims=True)
    y = (y - mu_y) * lax.rsqrt(var_y + GN_EPS) * hvec(lnx_g) + hvec(lnx_b)
    y = y + jnp.sum(r * k * r_k.astype(f32), axis=-1, keepdims=True) * v
    return y.reshape(bsz, seq, C_WIDTH).astype(z.dtype)


def setup_inputs(seed: int = 0) -> dict:
    key = jax.random.key(seed)
    ks = jax.random.split(key, 40)
    f32 = jnp.float32

    def nrm(k, shape, scale):
        return jax.random.normal(k, shape, f32) * scale

    def gain(k, shape):
        return 1.0 + 0.05 * jax.random.normal(k, shape, f32)

    L = DEPTH
    return {
        'x': nrm(ks[0], (BATCH, SEQ, D_MODEL), 1.0),
        'p': nrm(ks[1], (DEPTH, BATCH, SEQ, PLE_DIM), 1.0),
        'pre_norm_g': gain(ks[2], (L, D_MODEL)),
        'w_in': nrm(ks[3], (L, D_MODEL, IN_COLS), D_MODEL ** -0.5),
        'sgu_ln_g': gain(ks[4], (L, A_WIDTH)),
        'sgu_ln_b': nrm(ks[5], (L, A_WIDTH), 0.01),
        'sgu_w': nrm(ks[6], (L, A_HEADS, SGU_BLOCK, SGU_BLOCK), SGU_BLOCK ** -0.5),
        'sgu_b': 1.0 + nrm(ks[7], (L, A_HEADS, SGU_BLOCK), 0.1),
        'conv_w': nrm(ks[8], (L, CONV_WIDTH, B_WIDTH), CONV_WIDTH ** -0.5),
        'conv_b': nrm(ks[9], (L, B_WIDTH), 0.01),
        'conv_ln_g': gain(ks[10], (L, B_WIDTH)),
        'conv_ln_b': nrm(ks[11], (L, B_WIDTH), 0.01),
        'pw_w': nrm(ks[12], (L, B_WIDTH, B_WIDTH), B_WIDTH ** -0.5),
        'pw_b': nrm(ks[13], (L, B_WIDTH), 0.01),
        'shift_mu': jax.random.uniform(ks[14], (L, C_SHIFT_COLS), f32),
        'w0': nrm(ks[15], (L, C_WIDTH), 0.5),
        'w_up': nrm(ks[16], (L, DECAY_LORA, C_WIDTH), 0.5 * DECAY_LORA ** -0.5),
        'a0': nrm(ks[17], (L, C_WIDTH), 0.5),
        'a_up': nrm(ks[18], (L, ICLR_LORA, C_WIDTH), 0.5 * ICLR_LORA ** -0.5),
        'k_k': 0.85 + nrm(ks[19], (L, C_WIDTH), 0.05),
        'k_a': gain(ks[20], (L, C_WIDTH)),
        'r_k': nrm(ks[21], (L, C_HEADS, C_HEAD_DIM), 0.1),
        'lnx_g': gain(ks[22], (L, C_WIDTH)),
        'lnx_b': nrm(ks[23], (L, C_WIDTH), 0.01),
        'w_out': nrm(ks[24], (L, MIX_WIDTH, D_MODEL), MIX_WIDTH ** -0.5),
        'post_norm_g': gain(ks[25], (L, D_MODEL)),
        'ple_w': nrm(ks[26], (L, PLE_DIM, D_MODEL), PLE_DIM ** -0.5),
        'ple_gate_w': nrm(ks[27], (L, D_MODEL, D_MODEL), D_MODEL ** -0.5),
        'ple_gate_b': nrm(ks[28], (L, D_MODEL), 0.01),
    }


def reference(x, p, pre_norm_g, w_in, sgu_ln_g, sgu_ln_b, sgu_w, sgu_b, conv_w, conv_b,
              conv_ln_g, conv_ln_b, pw_w, pw_b, shift_mu, w0, w_up, a0, a_up, k_k, k_a,
              r_k, lnx_g, lnx_b, w_out, post_norm_g, ple_w, ple_gate_w, ple_gate_b):
    oa = 0
    ob = 3 * A_WIDTH
    oc = ob + 3 * B_WIDTH
    og = oc + C_SHIFT_COLS
    for i in range(DEPTH):
        h = rms_norm(x, pre_norm_g[i])
        z = h @ w_in[i]
        u_a = z[..., oa:oa + A_WIDTH]
        v_a = z[..., oa + A_WIDTH:oa + 2 * A_WIDTH]
        g_a = z[..., oa + 2 * A_WIDTH:ob]
        out_a = spatial_gating(u_a, v_a, sgu_ln_g[i], sgu_ln_b[i], sgu_w[i], sgu_b[i]) * jax.nn.silu(g_a)
        val_b = z[..., ob:ob + B_WIDTH]
        glu_b = z[..., ob + B_WIDTH:ob + 2 * B_WIDTH]
        g_b = z[..., ob + 2 * B_WIDTH:oc]
        out_b = conv_module(val_b, glu_b, conv_w[i], conv_b[i], conv_ln_g[i], conv_ln_b[i],
                            pw_w[i], pw_b[i]) * jax.nn.silu(g_b)
        out_c = rwkv7_mix(z[..., oc:og], shift_mu[i], w0[i], w_up[i], a0[i], a_up[i], k_k[i],
                          k_a[i], r_k[i], lnx_g[i], lnx_b[i]) * jax.nn.silu(z[..., og:])
        mix = jnp.concatenate([out_a, out_b, out_c], axis=-1) @ w_out[i]
        x = x + rms_norm(mix, post_norm_g[i])
        gate = jax.nn.sigmoid(x @ ple_gate_w[i] + ple_gate_b[i])
        x = x + (p[i] @ ple_w[i]) * gate
    return x
```

```python
import functools
import math

import jax
import jax.numpy as jnp
from jax import lax
from jax.experimental import pallas as pl
from jax.experimental.pallas import tpu as pltpu

F32 = jnp.float32
BF16 = jnp.bfloat16

D_MODEL = 1024
A_WIDTH = 256
B_WIDTH = 256
C_WIDTH = 512
HEAD_DIM = 64
LORA = 64
CONV_WIDTH = 31
PLE_DIM = 256
SGU_BLOCK = 128
SGU_CHUNK = 64
OFF_A = 0
OFF_B = 3 * A_WIDTH
OFF_C = OFF_B + 3 * B_WIDTH
C_SHIFT_COLS = 3 * C_WIDTH + 2 * LORA
OFF_G = OFF_C + C_SHIFT_COLS
IN_COLS = OFF_G + C_WIDTH
RMS_EPS = 1e-6
LN_EPS = 1e-5
GN_EPS = 64e-5
DECAY_SCALE = math.exp(-0.5)

LANES = 128
SUBLANES = 8
SCAN_CHUNK = 64
NEUMANN_ROUNDS = 5
SEQ_TILE = 256
CONV_HIST = 32
CONV_ROWS = 64
VMEM_LIMIT_BYTES = 56 * 1024 * 1024


def _dot(a, b):
    return jnp.dot(a, b, preferred_element_type=F32)


def _dot_nt(a, b):
    return lax.dot_general(a, b, (((1,), (1,)), ((), ())), preferred_element_type=F32)


def _dot_tn(a, b):
    return lax.dot_general(a, b, (((0,), (0,)), ((), ())), preferred_element_type=F32)


def _split(x):
    hi = x.astype(BF16)
    lo = (x - hi.astype(F32)).astype(BF16)
    return hi, lo


def _dot_hl(x, w):
    hi, lo = _split(x)
    return _dot(hi, w) + _dot(lo, w)


def _sigmoid(x):
    return jax.nn.sigmoid(x)


def _silu(x):
    return x * jax.nn.sigmoid(x)


def _layer_norm(x, g, b):
    mu = jnp.mean(x, axis=-1, keepdims=True)
    xc = x - mu
    var = jnp.mean(xc * xc, axis=-1, keepdims=True)
    return xc * lax.rsqrt(var + LN_EPS) * g + b


def _rms_norm(x, g):
    return x * lax.rsqrt(jnp.mean(x * x, axis=-1, keepdims=True) + RMS_EPS) * g


def _stack_heads(xp):
    lo = lax.broadcasted_iota(jnp.int32, xp.shape, 1) < HEAD_DIM
    zero = jnp.zeros_like(xp)
    return jnp.concatenate([jnp.where(lo, xp, zero), jnp.where(lo, zero, xp)], axis=0)


def _layer_kernel(x_ref, p_ref, win_ref, wout_ref, gatew_ref, plew_ref, sguw_ref, sgub_ref,
                  convw_ref, pww_ref, lora_ref, seg_ref, tri_ref, v1024_ref, v256_ref,
                  v512_ref, mu_ref, o_ref,
                  ybuf, zbuf, state, r_s, k_s, v_s, a_s, b_s, lw_s, cum_s, y_s):
    ts = x_ref.shape[1]
    t = SCAN_CHUNK

    @pl.when(pl.program_id(1) == 0)
    def _reset():
        ybuf[0:CONV_HIST, :] = jnp.zeros((CONV_HIST, B_WIDTH), F32)
        zbuf[0:SUBLANES, :] = jnp.zeros((SUBLANES, C_SHIFT_COLS), F32)
        state[...] = jnp.zeros(state.shape, F32)

    x = x_ref[0]
    pre_g = v1024_ref[0:1, :]
    post_g = v1024_ref[1:2, :]
    gate_b = v1024_ref[2:3, :]
    hb = _rms_norm(x, pre_g).astype(BF16)

    z_a = _dot(hb, win_ref[:, OFF_A:OFF_B])
    u_a = z_a[:, 0:A_WIDTH]
    v_a = _layer_norm(z_a[:, A_WIDTH:2 * A_WIDTH], v256_ref[0:1, :], v256_ref[1:2, :])
    g_a = z_a[:, 2 * A_WIDTH:3 * A_WIDTH]
    wi = lax.broadcasted_iota(jnp.int32, sguw_ref.shape, 0) // SGU_CHUNK
    wj = (lax.broadcasted_iota(jnp.int32, sguw_ref.shape, 1) % SGU_BLOCK) // SGU_CHUNK
    w_sgu = jnp.where(wj <= wi, sguw_ref[...], 0.0).astype(BF16)
    mixed_rows = []
    for nb in range(ts // SGU_BLOCK):
        vb = v_a[nb * SGU_BLOCK:(nb + 1) * SGU_BLOCK, :]
        pairs = []
        for pr in range(A_WIDTH // LANES):
            st = _stack_heads(vb[:, pr * LANES:(pr + 1) * LANES]).astype(BF16)
            pairs.append(_dot(w_sgu[:, 2 * pr * SGU_BLOCK:(2 * pr + 2) * SGU_BLOCK], st))
        mixed_rows.append(jnp.concatenate(pairs, axis=1) + sgub_ref[...])
    mixed = jnp.concatenate(mixed_rows, axis=0)
    out_a = (u_a * mixed * _silu(g_a)).astype(BF16)

    z_b = _dot(hb, win_ref[:, OFF_B:OFF_C])
    g_b = z_b[:, 2 * B_WIDTH:3 * B_WIDTH]
    ybuf[CONV_HIST:CONV_HIST + ts, :] = z_b[:, 0:B_WIDTH] * _sigmoid(z_b[:, B_WIDTH:2 * B_WIDTH])
    conv_rows = []
    first = CONV_HIST - (CONV_WIDTH - 1)
    for rt in range(ts // CONV_ROWS):
        acc = jnp.zeros((CONV_ROWS, B_WIDTH), F32) + v256_ref[2:3, :]
        for j in range(CONV_WIDTH):
            acc = acc + convw_ref[j:j + 1, :] * ybuf[pl.ds(rt * CONV_ROWS + first + j, CONV_ROWS), :]
        conv_rows.append(acc)
    ybuf[0:CONV_HIST, :] = ybuf[ts:ts + CONV_HIST, :]
    yc = _silu(_layer_norm(jnp.concatenate(conv_rows, axis=0), v256_ref[3:4, :], v256_ref[4:5, :]))
    out_b = ((_dot(yc.astype(BF16), pww_ref[...]) + v256_ref[5:6, :]) * _silu(g_b)).astype(BF16)

    zbuf[SUBLANES:SUBLANES + ts, :] = _dot(hb, win_ref[:, OFF_C:OFF_G])
    zc = zbuf[SUBLANES:SUBLANES + ts, :]
    zs = zc + mu_ref[...] * (zbuf[SUBLANES - 1:SUBLANES - 1 + ts, :] - zc)
    zbuf[0:SUBLANES, :] = zbuf[ts:ts + SUBLANES, :]
    g_c = _dot(hb, win_ref[:, OFF_G:IN_COLS])

    r = zs[:, 0:C_WIDTH]
    k = zs[:, C_WIDTH:2 * C_WIDTH]
    v = zs[:, 2 * C_WIDTH:3 * C_WIDTH]
    lora_in = zs[:, 3 * C_WIDTH:C_SHIFT_COLS]
    is_decay = lax.broadcasted_iota(jnp.int32, lora_in.shape, 1) < LORA
    lora_in = jnp.where(is_decay, jnp.tanh(lora_in), lora_in).astype(BF16)
    lora_out = _dot(lora_in, lora_ref[...])
    w0, a0 = v512_ref[0:1, :], v512_ref[1:2, :]
    k_k, k_a, r_k = v512_ref[2:3, :], v512_ref[3:4, :], v512_ref[4:5, :]
    lnx_g, lnx_b = v512_ref[5:6, :], v512_ref[6:7, :]
    log_w = -DECAY_SCALE * _sigmoid(w0 + lora_out[:, 0:C_WIDTH])
    iclr = _sigmoid(a0 + lora_out[:, C_WIDTH:2 * C_WIDTH])
    seg = seg_ref[...]
    kk = k * k_k
    kk = kk * lax.rsqrt(jnp.maximum(_dot_hl(kk * kk, seg), 1e-12))
    k = k * (1.0 + (iclr - 1.0) * k_a)
    bonus = _dot_hl(r * k * r_k, seg) * v
    log_hi, log_lo = _split(log_w)
    r_s[...] = r
    k_s[...] = k
    v_s[...] = v
    a_s[...] = -kk
    b_s[...] = kk * iclr
    lw_s[...] = log_w
    cum_s[...] = _dot(tri_ref[...], log_hi) + _dot(tri_ref[...], log_lo)

    row = lax.broadcasted_iota(jnp.int32, (2 * t, 2 * t), 0) % t
    col = lax.broadcasted_iota(jnp.int32, (2 * t, 2 * t), 1) % t
    strict = row > col
    incl = row >= col
    eye = (lax.broadcasted_iota(jnp.int32, (2 * t, 2 * t), 0)
           == lax.broadcasted_iota(jnp.int32, (2 * t, 2 * t), 1)).astype(F32)

    def chunk(c, carry):
        base = pl.multiple_of(c * t, t)
        rows = pl.ds(base, t)
        cum = cum_s[rows, :]
        mid = cum_s[pl.ds(base + t // 2 - 1, 1), :]
        end = cum_s[pl.ds(base + t - 1, 1), :]
        e_mid = jnp.exp(-mid)
        e_end = jnp.exp(end)
        e_end_mid = jnp.exp(end - mid)
        e_inv = jnp.exp(mid - cum)
        r_abs = r_s[rows, :] * jnp.exp(cum)
        a_abs = a_s[rows, :] * jnp.exp(cum - lw_s[rows, :])
        r_rel = r_abs * e_mid
        a_rel = a_abs * e_mid
        k_rel = k_s[rows, :] * e_inv
        b_rel = b_s[rows, :] * e_inv
        k_end = k_rel * e_end_mid
        b_end = b_rel * e_end_mid
        vals = v_s[rows, :]
        for pr in range(C_WIDTH // LANES):
            ln = slice(pr * LANES, (pr + 1) * LANES)

            def st(arr):
                return _stack_heads(arr[:, ln]).astype(BF16)

            v_st = st(vals)
            scores = _dot_nt(jnp.concatenate([st(a_rel), st(r_rel)], axis=0),
                             jnp.concatenate([st(b_rel), st(k_rel)], axis=0))
            l_ab = jnp.where(strict, scores[0:2 * t, 0:2 * t], 0.0)
            l_ak = jnp.where(strict, scores[0:2 * t, 2 * t:4 * t], 0.0)
            a_rb = jnp.where(incl, scores[2 * t:4 * t, 0:2 * t], 0.0)
            a_rk = jnp.where(incl, scores[2 * t:4 * t, 2 * t:4 * t], 0.0)
            inv = eye + l_ab
            power = l_ab
            for _ in range(NEUMANN_ROUNDS):
                pb = power.astype(BF16)
                power = _dot(pb, pb)
                inv = inv + _dot(inv.astype(BF16), power.astype(BF16))
            s_old = state[pr]
            from_state = _dot_nt(jnp.concatenate([st(a_abs), st(r_abs)], axis=0), s_old.astype(BF16))
            rhs = from_state[0:2 * t, :] + _dot(l_ak.astype(BF16), v_st)
            u_st = _dot(inv.astype(BF16), rhs.astype(BF16)).astype(BF16)
            uv = jnp.concatenate([u_st, v_st], axis=0)
            y_st = from_state[2 * t:4 * t, :] + _dot(
                jnp.concatenate([a_rb, a_rk], axis=1).astype(BF16), uv)
            y_s[rows, ln] = y_st[0:t, :] + y_st[t:2 * t, :]
            state[pr] = s_old * e_end[:, ln] + _dot_tn(
                uv, jnp.concatenate([st(b_end), st(k_end)], axis=0))
        return carry

    lax.fori_loop(0, ts // t, chunk, 0)

    y = y_s[...]
    inv_n = 1.0 / HEAD_DIM
    yc = y - _dot_hl(y, seg) * inv_n
    var = _dot_hl(yc * yc, seg) * inv_n
    y = yc * lax.rsqrt(var + GN_EPS) * lnx_g + lnx_b + bonus
    out_c = (y * _silu(g_c)).astype(BF16)

    mix = (_dot(out_a, wout_ref[0:A_WIDTH, :])
           + _dot(out_b, wout_ref[A_WIDTH:A_WIDTH + B_WIDTH, :])
           + _dot(out_c, wout_ref[A_WIDTH + B_WIDTH:, :]))
    x1 = x + _rms_norm(mix, post_g)
    gate = _sigmoid(_dot(x1.astype(BF16), gatew_ref[...]) + gate_b)
    ple = _dot(p_ref[0, 0].astype(BF16), plew_ref[...])
    o_ref[0] = x1 + ple * gate


def _pad_rows(rows, width):
    arr = jnp.stack(rows).astype(F32)
    return jnp.pad(arr, ((0, SUBLANES - arr.shape[0]), (0, 0)))


def _layer_call(i, x, p, consts, weights):
    bsz, seq, _ = x.shape
    ts = min(SEQ_TILE, seq)
    assert seq % ts == 0 and ts % SGU_BLOCK == 0 and ts % SCAN_CHUNK == 0 and ts % CONV_ROWS == 0
    seg, tri = consts

    def full(arr):
        nd = arr.ndim
        return pl.BlockSpec(arr.shape, lambda b, s, _nd=nd: (0,) * _nd)

    ins = [x, p] + list(weights[:9]) + [seg, tri] + list(weights[9:])
    in_specs = [pl.BlockSpec((1, ts, D_MODEL), lambda b, s: (b, s, 0)),
                pl.BlockSpec((1, 1, ts, PLE_DIM), lambda b, s, _i=i: (_i, b, s, 0))]
    in_specs += [full(a) for a in ins[2:]]
    scratch = [
        pltpu.VMEM((ts + CONV_HIST, B_WIDTH), F32),
        pltpu.VMEM((ts + SUBLANES, C_SHIFT_COLS), F32),
        pltpu.VMEM((C_WIDTH // LANES, LANES, LANES), F32),
    ] + [pltpu.VMEM((ts, C_WIDTH), F32) for _ in range(8)]
    return pl.pallas_call(
        _layer_kernel,
        grid=(bsz, seq // ts),
        in_specs=in_specs,
        out_specs=pl.BlockSpec((1, ts, D_MODEL), lambda b, s: (b, s, 0)),
        out_shape=jax.ShapeDtypeStruct(x.shape, F32),
        scratch_shapes=scratch,
        compiler_params=pltpu.CompilerParams(
            dimension_semantics=("arbitrary", "arbitrary"),
            vmem_limit_bytes=VMEM_LIMIT_BYTES),
        name=f"layer{i}",
    )(*ins)


def kernel(x, p, pre_norm_g, w_in, sgu_ln_g, sgu_ln_b, sgu_w, sgu_b, conv_w, conv_b, conv_ln_g,
           conv_ln_b, pw_w, pw_b, shift_mu, w0, w_up, a0, a_up, k_k, k_a, r_k, lnx_g, lnx_b,
           w_out, post_norm_g, ple_w, ple_gate_w, ple_gate_b):
    depth = w_in.shape[0]
    seq = x.shape[1]
    ts = min(SEQ_TILE, seq)
    head = jnp.arange(C_WIDTH) // HEAD_DIM
    seg = (head[:, None] == head[None, :]).astype(BF16)
    pos = jnp.arange(ts)
    tri = ((pos[:, None] // SCAN_CHUNK == pos[None, :] // SCAN_CHUNK)
           & (pos[None, :] <= pos[:, None])).astype(BF16)
    zeros = jnp.zeros((LORA, C_WIDTH), F32)
    for i in range(depth):
        lora = jnp.concatenate([jnp.concatenate([w_up[i], zeros], axis=1),
                                jnp.concatenate([zeros, a_up[i]], axis=1)], axis=0).astype(BF16)
        weights = (
            w_in[i].astype(BF16), w_out[i].astype(BF16), ple_gate_w[i].astype(BF16),
            ple_w[i].astype(BF16),
            sgu_w[i].transpose(1, 0, 2).reshape(SGU_BLOCK, -1),
            jnp.repeat(sgu_b[i].T, HEAD_DIM, axis=1),
            jnp.pad(conv_w[i], ((0, 1), (0, 0))),
            pw_w[i].astype(BF16), lora,
            _pad_rows([pre_norm_g[i], post_norm_g[i], ple_gate_b[i]], D_MODEL),
            _pad_rows([sgu_ln_g[i], sgu_ln_b[i], conv_b[i], conv_ln_g[i], conv_ln_b[i], pw_b[i]],
                      B_WIDTH),
            _pad_rows([w0[i], a0[i], k_k[i], k_a[i], r_k[i].reshape(-1), lnx_g[i], lnx_b[i]],
                      C_WIDTH),
            shift_mu[i].reshape(1, -1),
        )
        x = _layer_call(i, x, p, (seg, tri), weights)
    return x
```

```python
import functools
import math

import jax
import jax.numpy as jnp
from jax import lax
from jax.experimental import pallas as pl
from jax.experimental.pallas import tpu as pltpu

F32 = jnp.float32
BF16 = jnp.bfloat16

D_MODEL = 1024
A_WIDTH = 256
B_WIDTH = 256
C_WIDTH = 512
HEAD_DIM = 64
LORA = 64
CONV_WIDTH = 31
PLE_DIM = 256
SGU_BLOCK = 128
SGU_CHUNK = 64
OFF_A = 0
OFF_B = 3 * A_WIDTH
OFF_C = OFF_B + 3 * B_WIDTH
C_SHIFT_COLS = 3 * C_WIDTH + 2 * LORA
OFF_G = OFF_C + C_SHIFT_COLS
IN_COLS = OFF_G + C_WIDTH
RMS_EPS = 1e-6
LN_EPS = 1e-5
GN_EPS = 64e-5
DECAY_SCALE = math.exp(-0.5)

LANES = 128
SUBLANES = 8
SCAN_CHUNK = 64
NEUMANN_FACTORS = 6
assert SCAN_CHUNK == HEAD_DIM
SEQ_TILE = 256
CONV_HIST = 32
CONV_ROWS = 64
VMEM_LIMIT_BYTES = 56 * 1024 * 1024


def _dot(a, b):
    return jnp.dot(a, b, preferred_element_type=F32)


def _dot_nt(a, b):
    return lax.dot_general(a, b, (((1,), (1,)), ((), ())), preferred_element_type=F32)


def _dot_tn(a, b):
    return lax.dot_general(a, b, (((0,), (0,)), ((), ())), preferred_element_type=F32)


def _split(x):
    hi = x.astype(BF16)
    lo = (x - hi.astype(F32)).astype(BF16)
    return hi, lo


def _dot_hl(x, w):
    hi, lo = _split(x)
    return _dot(hi, w) + _dot(lo, w)


def _sigmoid(x):
    return jax.nn.sigmoid(x)


def _silu(x):
    return x * jax.nn.sigmoid(x)


def _layer_norm(x, g, b):
    mu = jnp.mean(x, axis=-1, keepdims=True)
    xc = x - mu
    var = jnp.mean(xc * xc, axis=-1, keepdims=True)
    return xc * lax.rsqrt(var + LN_EPS) * g + b


def _rms_norm(x, g):
    return x * lax.rsqrt(jnp.mean(x * x, axis=-1, keepdims=True) + RMS_EPS) * g


def _stack_heads(xp):
    lo = lax.broadcasted_iota(jnp.int32, xp.shape, 1) < HEAD_DIM
    zero = jnp.zeros_like(xp)
    return jnp.concatenate([jnp.where(lo, xp, zero), jnp.where(lo, zero, xp)], axis=0)


def _layer_kernel(x_ref, p_ref, win_ref, wout_ref, gatew_ref, plew_ref, sguw_ref, sgub_ref,
                  convw_ref, pww_ref, lora_ref, seg_ref, tri_ref, v1024_ref, v256_ref,
                  v512_ref, mu_ref, o_ref,
                  ybuf, zbuf, state):
    ts = x_ref.shape[1]
    t = SCAN_CHUNK

    @pl.when(pl.program_id(1) == 0)
    def _reset():
        ybuf[0:CONV_HIST, :] = jnp.zeros((CONV_HIST, B_WIDTH), F32)
        zbuf[0:SUBLANES, :] = jnp.zeros((SUBLANES, C_SHIFT_COLS), F32)
        state[...] = jnp.zeros(state.shape, F32)

    x = x_ref[0]
    pre_g = v1024_ref[0:1, :]
    post_g = v1024_ref[1:2, :]
    gate_b = v1024_ref[2:3, :]
    hb = _rms_norm(x, pre_g).astype(BF16)

    z_a = _dot(hb, win_ref[:, OFF_A:OFF_B])
    u_a = z_a[:, 0:A_WIDTH]
    v_a = _layer_norm(z_a[:, A_WIDTH:2 * A_WIDTH], v256_ref[0:1, :], v256_ref[1:2, :])
    g_a = z_a[:, 2 * A_WIDTH:3 * A_WIDTH]
    wi = lax.broadcasted_iota(jnp.int32, sguw_ref.shape, 0) // SGU_CHUNK
    wj = (lax.broadcasted_iota(jnp.int32, sguw_ref.shape, 1) % SGU_BLOCK) // SGU_CHUNK
    w_sgu = jnp.where(wj <= wi, sguw_ref[...], 0.0).astype(BF16)
    mixed_rows = []
    for nb in range(ts // SGU_BLOCK):
        vb = v_a[nb * SGU_BLOCK:(nb + 1) * SGU_BLOCK, :]
        pairs = []
        for pr in range(A_WIDTH // LANES):
            st = _stack_heads(vb[:, pr * LANES:(pr + 1) * LANES]).astype(BF16)
            pairs.append(_dot(w_sgu[:, 2 * pr * SGU_BLOCK:(2 * pr + 2) * SGU_BLOCK], st))
        mixed_rows.append(jnp.concatenate(pairs, axis=1) + sgub_ref[...])
    mixed = jnp.concatenate(mixed_rows, axis=0)
    out_a = (u_a * mixed * _silu(g_a)).astype(BF16)

    z_b = _dot(hb, win_ref[:, OFF_B:OFF_C])
    g_b = z_b[:, 2 * B_WIDTH:3 * B_WIDTH]
    ybuf[CONV_HIST:CONV_HIST + ts, :] = z_b[:, 0:B_WIDTH] * _sigmoid(z_b[:, B_WIDTH:2 * B_WIDTH])
    conv_rows = []
    first = CONV_HIST - (CONV_WIDTH - 1)
    for rt in range(ts // CONV_ROWS):
        acc = jnp.zeros((CONV_ROWS, B_WIDTH), F32) + v256_ref[2:3, :]
        for j in range(CONV_WIDTH):
            acc = acc + convw_ref[j:j + 1, :] * ybuf[pl.ds(rt * CONV_ROWS + first + j, CONV_ROWS), :]
        conv_rows.append(acc)
    ybuf[0:CONV_HIST, :] = ybuf[ts:ts + CONV_HIST, :]
    yc = _silu(_layer_norm(jnp.concatenate(conv_rows, axis=0), v256_ref[3:4, :], v256_ref[4:5, :]))
    out_b = ((_dot(yc.astype(BF16), pww_ref[...]) + v256_ref[5:6, :]) * _silu(g_b)).astype(BF16)

    zbuf[SUBLANES:SUBLANES + ts, :] = _dot(hb, win_ref[:, OFF_C:OFF_G])
    zc = zbuf[SUBLANES:SUBLANES + ts, :]
    zs = zc + mu_ref[...] * (zbuf[SUBLANES - 1:SUBLANES - 1 + ts, :] - zc)
    zbuf[0:SUBLANES, :] = zbuf[ts:ts + SUBLANES, :]
    g_c = _dot(hb, win_ref[:, OFF_G:IN_COLS])

    r = zs[:, 0:C_WIDTH]
    k = zs[:, C_WIDTH:2 * C_WIDTH]
    v = zs[:, 2 * C_WIDTH:3 * C_WIDTH]
    lora_in = zs[:, 3 * C_WIDTH:C_SHIFT_COLS]
    is_decay = lax.broadcasted_iota(jnp.int32, lora_in.shape, 1) < LORA
    lora_in = jnp.where(is_decay, jnp.tanh(lora_in), lora_in).astype(BF16)
    lora_out = _dot(lora_in, lora_ref[...])
    w0, a0 = v512_ref[0:1, :], v512_ref[1:2, :]
    k_k, k_a, r_k = v512_ref[2:3, :], v512_ref[3:4, :], v512_ref[4:5, :]
    lnx_g, lnx_b = v512_ref[5:6, :], v512_ref[6:7, :]
    log_w = -DECAY_SCALE * _sigmoid(w0 + lora_out[:, 0:C_WIDTH])
    iclr = _sigmoid(a0 + lora_out[:, C_WIDTH:2 * C_WIDTH])
    seg = seg_ref[...]
    kk = k * k_k
    kk = kk * lax.rsqrt(jnp.maximum(_dot_hl(kk * kk, seg), 1e-12))
    k = k * (1.0 + (iclr - 1.0) * k_a)
    bonus = _dot_hl(r * k * r_k, seg) * v
    log_hi, log_lo = _split(log_w)
    cum_all = _dot(tri_ref[...], log_hi) + _dot(tri_ref[...], log_lo)
    a_vec = -kk
    b_vec = kk * iclr

    row = lax.broadcasted_iota(jnp.int32, (t, 2 * t), 0)
    col = lax.broadcasted_iota(jnp.int32, (t, 2 * t), 1) % t
    strict = col < row
    incl = col <= row
    eye = (col == row).astype(F32)
    same_head = (lax.broadcasted_iota(jnp.int32, (LANES, LANES), 0) // HEAD_DIM
                 == lax.broadcasted_iota(jnp.int32, (LANES, LANES), 1) // HEAD_DIM)
    n_chunks = ts // t
    n_pairs = C_WIDTH // LANES
    combos = [(c, pr) for c in range(n_chunks) for pr in range(n_pairs)]

    def bf(arr):
        return arr.astype(BF16)

    def stk(arr):
        return _stack_heads(bf(arr))

    r_abs, a_abs, r_rel, a_rel, k_rel, b_rel, k_end, b_end, vals, e_end = ({} for _ in range(10))
    for c in range(n_chunks):
        rows = slice(c * t, (c + 1) * t)
        cum = cum_all[rows, :]
        mid = cum_all[c * t + t // 2 - 1:c * t + t // 2, :]
        end = cum_all[(c + 1) * t - 1:(c + 1) * t, :]
        e_mid = jnp.exp(-mid)
        e_end_mid = jnp.exp(end - mid)
        e_inv = jnp.exp(mid - cum)
        e_end[c] = jnp.exp(end)
        r_abs[c] = r[rows, :] * jnp.exp(cum)
        a_abs_c = a_vec[rows, :] * jnp.exp(cum - log_w[rows, :])
        k_rel_c = k[rows, :] * e_inv
        b_rel_c = b_vec[rows, :] * e_inv
        a_abs[c] = bf(a_abs_c)
        r_rel[c] = bf(r_abs[c] * e_mid)
        a_rel[c] = bf(a_abs_c * e_mid)
        k_rel[c] = bf(k_rel_c)
        b_rel[c] = bf(b_rel_c)
        k_end[c] = bf(k_rel_c * e_end_mid)
        b_end[c] = bf(b_rel_c * e_end_mid)
        vals[c] = bf(v[rows, :])

    def lanes(pr):
        return slice(pr * LANES, (pr + 1) * LANES)

    v_st, l_ak, a_rb, a_rk, inv, power = ({} for _ in range(6))
    for c, pr in combos:
        ln = lanes(pr)
        v_st[c, pr] = _stack_heads(vals[c][:, ln])
        scores = _dot_nt(
            jnp.concatenate([a_rel[c][:, ln], r_rel[c][:, ln]], axis=0),
            jnp.concatenate([_stack_heads(b_rel[c][:, ln]), _stack_heads(k_rel[c][:, ln])], axis=0))
        power[c, pr] = jnp.where(strict, scores[0:t, 0:2 * t], 0.0)
        l_ak[c, pr] = bf(jnp.where(strict, scores[0:t, 2 * t:4 * t], 0.0))
        a_rb[c, pr] = bf(jnp.where(incl, scores[t:2 * t, 0:2 * t], 0.0))
        a_rk[c, pr] = bf(jnp.where(incl, scores[t:2 * t, 2 * t:4 * t], 0.0))
        inv[c, pr] = eye + power[c, pr]
    for cp in combos:
        pb = bf(power[cp])
        power[cp] = _dot(pb, _stack_heads(pb))
    for _ in range(NEUMANN_FACTORS - 2):
        for cp in combos:
            both = _dot(bf(power[cp]), jnp.concatenate([stk(inv[cp]), stk(power[cp])], axis=1))
            inv[cp] = inv[cp] + both[:, 0:2 * t]
            power[cp] = both[:, 2 * t:4 * t]
    for cp in combos:
        inv[cp] = bf(inv[cp] + _dot(bf(power[cp]), stk(inv[cp])))

    lakv, w_loc, g_loc, y_loc, q_mat, phi, psi = ({} for _ in range(7))
    for cp in combos:
        lakv[cp] = _dot(l_ak[cp], v_st[cp])
    for c, pr in combos:
        both = _dot(inv[c, pr],
                    jnp.concatenate([stk(lakv[c, pr]), _stack_heads(a_abs[c][:, lanes(pr)])], axis=1))
        w_loc[c, pr] = bf(both[:, 0:LANES])
        g_loc[c, pr] = bf(both[:, LANES:2 * LANES])
    for c, pr in combos:
        cp = (c, pr)
        rhs = jnp.concatenate(
            [jnp.concatenate([_stack_heads(w_loc[cp]), _stack_heads(g_loc[cp])], axis=1),
             jnp.concatenate([v_st[cp], jnp.zeros_like(v_st[cp])], axis=1)], axis=0)
        both = _dot(jnp.concatenate([a_rb[cp], a_rk[cp]], axis=1), rhs)
        y_loc[cp] = both[:, 0:LANES]
        q_mat[cp] = bf(r_abs[c][:, lanes(pr)] + both[:, LANES:2 * LANES])
    for c, pr in combos:
        cp = (c, pr)
        ln = lanes(pr)
        phi[cp] = bf(jnp.where(same_head, _dot_tn(g_loc[cp], b_end[c][:, ln]), 0.0))
        psi[cp] = jnp.where(
            same_head,
            _dot_tn(jnp.concatenate([w_loc[cp], vals[c][:, ln]], axis=0),
                    jnp.concatenate([b_end[c][:, ln], k_end[c][:, ln]], axis=0)), 0.0)

    s_start = {}
    s_cur = [state[pr] for pr in range(n_pairs)]
    for c in range(n_chunks):
        for pr in range(n_pairs):
            s_bf = bf(s_cur[pr])
            s_start[c, pr] = s_bf
            s_cur[pr] = s_cur[pr] * e_end[c][:, lanes(pr)] + _dot(s_bf, phi[c, pr]) + psi[c, pr]
    for pr in range(n_pairs):
        state[pr] = s_cur[pr]
    y = jnp.concatenate(
        [jnp.concatenate([_dot_nt(q_mat[c, pr], s_start[c, pr]) + y_loc[c, pr]
                          for pr in range(n_pairs)], axis=1) for c in range(n_chunks)], axis=0)

    inv_n = 1.0 / HEAD_DIM
    yc = y - _dot_hl(y, seg) * inv_n
    var = _dot_hl(yc * yc, seg) * inv_n
    y = yc * lax.rsqrt(var + GN_EPS) * lnx_g + lnx_b + bonus
    out_c = (y * _silu(g_c)).astype(BF16)

    mix = (_dot(out_a, wout_ref[0:A_WIDTH, :])
           + _dot(out_b, wout_ref[A_WIDTH:A_WIDTH + B_WIDTH, :])
           + _dot(out_c, wout_ref[A_WIDTH + B_WIDTH:, :]))
    x1 = x + _rms_norm(mix, post_g)
    gate = _sigmoid(_dot(x1.astype(BF16), gatew_ref[...]) + gate_b)
    ple = _dot(p_ref[0, 0].astype(BF16), plew_ref[...])
    o_ref[0] = x1 + ple * gate


def _pad_rows(rows, width):
    arr = jnp.stack(rows).astype(F32)
    return jnp.pad(arr, ((0, SUBLANES - arr.shape[0]), (0, 0)))


def _layer_call(i, x, p, consts, weights):
    bsz, seq, _ = x.shape
    ts = min(SEQ_TILE, seq)
    assert seq % ts == 0 and ts % SGU_BLOCK == 0 and ts % SCAN_CHUNK == 0 and ts % CONV_ROWS == 0
    seg, tri = consts

    def full(arr):
        nd = arr.ndim
        return pl.BlockSpec(arr.shape, lambda b, s, _nd=nd: (0,) * _nd)

    ins = [x, p] + list(weights[:9]) + [seg, tri] + list(weights[9:])
    in_specs = [pl.BlockSpec((1, ts, D_MODEL), lambda b, s: (b, s, 0)),
                pl.BlockSpec((1, 1, ts, PLE_DIM), lambda b, s, _i=i: (_i, b, s, 0))]
    in_specs += [full(a) for a in ins[2:]]
    scratch = [
        pltpu.VMEM((ts + CONV_HIST, B_WIDTH), F32),
        pltpu.VMEM((ts + SUBLANES, C_SHIFT_COLS), F32),
        pltpu.VMEM((C_WIDTH // LANES, LANES, LANES), F32),
    ]
    return pl.pallas_call(
        _layer_kernel,
        grid=(bsz, seq // ts),
        in_specs=in_specs,
        out_specs=pl.BlockSpec((1, ts, D_MODEL), lambda b, s: (b, s, 0)),
        out_shape=jax.ShapeDtypeStruct(x.shape, F32),
        scratch_shapes=scratch,
        compiler_params=pltpu.CompilerParams(
            dimension_semantics=("arbitrary", "arbitrary"),
            vmem_limit_bytes=VMEM_LIMIT_BYTES),
        name=f"layer{i}",
    )(*ins)


def kernel(x, p, pre_norm_g, w_in, sgu_ln_g, sgu_ln_b, sgu_w, sgu_b, conv_w, conv_b, conv_ln_g,
           conv_ln_b, pw_w, pw_b, shift_mu, w0, w_up, a0, a_up, k_k, k_a, r_k, lnx_g, lnx_b,
           w_out, post_norm_g, ple_w, ple_gate_w, ple_gate_b):
    depth = w_in.shape[0]
    seq = x.shape[1]
    ts = min(SEQ_TILE, seq)
    head = jnp.arange(C_WIDTH) // HEAD_DIM
    seg = (head[:, None] == head[None, :]).astype(BF16)
    pos = jnp.arange(ts)
    tri = ((pos[:, None] // SCAN_CHUNK == pos[None, :] // SCAN_CHUNK)
           & (pos[None, :] <= pos[:, None])).astype(BF16)
    zeros = jnp.zeros((LORA, C_WIDTH), F32)
    for i in range(depth):
        lora = jnp.concatenate([jnp.concatenate([w_up[i], zeros], axis=1),
                                jnp.concatenate([zeros, a_up[i]], axis=1)], axis=0).astype(BF16)
        weights = (
            w_in[i].astype(BF16), w_out[i].astype(BF16), ple_gate_w[i].astype(BF16),
            ple_w[i].astype(BF16),
            sgu_w[i].transpose(1, 0, 2).reshape(SGU_BLOCK, -1),
            jnp.repeat(sgu_b[i].T, HEAD_DIM, axis=1),
            jnp.pad(conv_w[i], ((0, 1), (0, 0))),
            pw_w[i].astype(BF16), lora,
            _pad_rows([pre_norm_g[i], post_norm_g[i], ple_gate_b[i]], D_MODEL),
            _pad_rows([sgu_ln_g[i], sgu_ln_b[i], conv_b[i], conv_ln_g[i], conv_ln_b[i], pw_b[i]],
                      B_WIDTH),
            _pad_rows([w0[i], a0[i], k_k[i], k_a[i], r_k[i].reshape(-1), lnx_g[i], lnx_b[i]],
                      C_WIDTH),
            shift_mu[i].reshape(1, -1),
        )
        x = _layer_call(i, x, p, (seg, tri), weights)
    return x
```

```python
import functools
import math

import jax
import jax.numpy as jnp
from jax import lax
from jax.experimental import pallas as pl
from jax.experimental.pallas import tpu as pltpu

F32 = jnp.float32
BF16 = jnp.bfloat16

D_MODEL = 1024
A_WIDTH = 256
B_WIDTH = 256
C_WIDTH = 512
HEAD_DIM = 64
LORA = 64
CONV_WIDTH = 31
PLE_DIM = 256
SGU_BLOCK = 128
SGU_CHUNK = 64
OFF_A = 0
OFF_B = 3 * A_WIDTH
OFF_C = OFF_B + 3 * B_WIDTH
C_SHIFT_COLS = 3 * C_WIDTH + 2 * LORA
OFF_G = OFF_C + C_SHIFT_COLS
IN_COLS = OFF_G + C_WIDTH
RMS_EPS = 1e-6
LN_EPS = 1e-5
GN_EPS = 64e-5
DECAY_SCALE = math.exp(-0.5)

LANES = 128
SUBLANES = 8
SCAN_CHUNK = 64
NEUMANN_FACTORS = 6
assert SCAN_CHUNK == HEAD_DIM
SEQ_TILE = 256
CONV_HIST = 32
CONV_ROWS = 64
SEG_WIDTH = 256
VMEM_LIMIT_BYTES = 56 * 1024 * 1024


def _dot(a, b):
    return jnp.dot(a, b, preferred_element_type=F32)


def _dot_nt(a, b):
    return lax.dot_general(a, b, (((1,), (1,)), ((), ())), preferred_element_type=F32)


def _dot_tn(a, b):
    return lax.dot_general(a, b, (((0,), (0,)), ((), ())), preferred_element_type=F32)


def _split(x):
    hi = x.astype(BF16)
    lo = (x - hi.astype(F32)).astype(BF16)
    return hi, lo


def _dot_hl(x, w):
    hi, lo = _split(x)
    return _dot(hi, w) + _dot(lo, w)


def _head_sums(x, seg):
    width = seg.shape[0]
    return jnp.concatenate([_dot_hl(x[:, lo:lo + width], seg) for lo in range(0, x.shape[1], width)],
                           axis=1)


def _sigmoid(x):
    return jax.nn.sigmoid(x)


def _silu(x):
    return x * jax.nn.sigmoid(x)


def _layer_norm(x, g, b):
    mu = jnp.mean(x, axis=-1, keepdims=True)
    xc = x - mu
    var = jnp.mean(xc * xc, axis=-1, keepdims=True)
    return xc * lax.rsqrt(var + LN_EPS) * g + b


def _rms_norm(x, g):
    return x * lax.rsqrt(jnp.mean(x * x, axis=-1, keepdims=True) + RMS_EPS) * g


def _stack_heads(xp):
    lo = lax.broadcasted_iota(jnp.int32, xp.shape, 1) < HEAD_DIM
    zero = jnp.zeros_like(xp)
    return jnp.concatenate([jnp.where(lo, xp, zero), jnp.where(lo, zero, xp)], axis=0)


def _conv_rows(ybuf, convw_ref, base):
    first = CONV_HIST - (CONV_WIDTH - 1)
    out = None
    for res in range(SUBLANES):
        extra = SUBLANES if res else 0
        part = None
        for j in range(CONV_WIDTH):
            if (first + j) % SUBLANES != res:
                continue
            term = convw_ref[j:j + 1, :] * ybuf[pl.ds(base + first + j - res, CONV_ROWS + extra), :]
            part = term if part is None else part + term
        if part is None:
            continue
        part = part[res:res + CONV_ROWS, :]
        out = part if out is None else out + part
    return out


def _layer_kernel(x_ref, p_ref, win_ref, wout_ref, gatew_ref, plew_ref, sguw_ref, sgub_ref,
                  convw_ref, pww_ref, lora_ref, seg_ref, tri_ref, v1024_ref, v256_ref,
                  v512_ref, mu_ref, o_ref,
                  ybuf, zbuf, state):
    ts = x_ref.shape[1]
    t = SCAN_CHUNK

    @pl.when(pl.program_id(1) == 0)
    def _reset():
        ybuf[0:CONV_HIST, :] = jnp.zeros((CONV_HIST, B_WIDTH), F32)
        zbuf[0:SUBLANES, :] = jnp.zeros((SUBLANES, C_SHIFT_COLS), F32)
        state[...] = jnp.zeros(state.shape, F32)

    x = x_ref[0]
    pre_g = v1024_ref[0:1, :]
    post_g = v1024_ref[1:2, :]
    gate_b = v1024_ref[2:3, :]
    hb = _rms_norm(x, pre_g).astype(BF16)

    z_a = _dot(hb, win_ref[:, OFF_A:OFF_B])
    u_a = z_a[:, 0:A_WIDTH]
    v_a = _layer_norm(z_a[:, A_WIDTH:2 * A_WIDTH], v256_ref[0:1, :], v256_ref[1:2, :])
    g_a = z_a[:, 2 * A_WIDTH:3 * A_WIDTH]
    wi = lax.broadcasted_iota(jnp.int32, sguw_ref.shape, 0) // SGU_CHUNK
    wj = (lax.broadcasted_iota(jnp.int32, sguw_ref.shape, 1) % SGU_BLOCK) // SGU_CHUNK
    w_sgu = jnp.where(wj <= wi, sguw_ref[...], 0.0).astype(BF16)
    mixed_rows = []
    for nb in range(ts // SGU_BLOCK):
        vb = v_a[nb * SGU_BLOCK:(nb + 1) * SGU_BLOCK, :]
        pairs = []
        for pr in range(A_WIDTH // LANES):
            st = _stack_heads(vb[:, pr * LANES:(pr + 1) * LANES]).astype(BF16)
            pairs.append(_dot(w_sgu[:, 2 * pr * SGU_BLOCK:(2 * pr + 2) * SGU_BLOCK], st))
        mixed_rows.append(jnp.concatenate(pairs, axis=1) + sgub_ref[...])
    mixed = jnp.concatenate(mixed_rows, axis=0)
    out_a = (u_a * mixed * _silu(g_a)).astype(BF16)

    z_b = _dot(hb, win_ref[:, OFF_B:OFF_C])
    g_b = z_b[:, 2 * B_WIDTH:3 * B_WIDTH]
    ybuf[CONV_HIST:CONV_HIST + ts, :] = z_b[:, 0:B_WIDTH] * _sigmoid(z_b[:, B_WIDTH:2 * B_WIDTH])
    conv_rows = []
    for rt in range(ts // CONV_ROWS):
        conv_rows.append(_conv_rows(ybuf, convw_ref, rt * CONV_ROWS) + v256_ref[2:3, :])
    ybuf[0:CONV_HIST, :] = ybuf[ts:ts + CONV_HIST, :]
    yc = _silu(_layer_norm(jnp.concatenate(conv_rows, axis=0), v256_ref[3:4, :], v256_ref[4:5, :]))
    out_b = ((_dot(yc.astype(BF16), pww_ref[...]) + v256_ref[5:6, :]) * _silu(g_b)).astype(BF16)

    zbuf[SUBLANES:SUBLANES + ts, :] = _dot(hb, win_ref[:, OFF_C:OFF_G])
    zc = zbuf[SUBLANES:SUBLANES + ts, :]
    zs = zc + mu_ref[...] * (zbuf[SUBLANES - 1:SUBLANES - 1 + ts, :] - zc)
    zbuf[0:SUBLANES, :] = zbuf[ts:ts + SUBLANES, :]
    g_c = _dot(hb, win_ref[:, OFF_G:IN_COLS])

    r = zs[:, 0:C_WIDTH]
    k = zs[:, C_WIDTH:2 * C_WIDTH]
    v = zs[:, 2 * C_WIDTH:3 * C_WIDTH]
    lora_in = zs[:, 3 * C_WIDTH:C_SHIFT_COLS]
    is_decay = lax.broadcasted_iota(jnp.int32, lora_in.shape, 1) < LORA
    lora_in = jnp.where(is_decay, jnp.tanh(lora_in), lora_in).astype(BF16)
    lora_out = _dot(lora_in, lora_ref[...])
    w0, a0 = v512_ref[0:1, :], v512_ref[1:2, :]
    k_k, k_a, r_k = v512_ref[2:3, :], v512_ref[3:4, :], v512_ref[4:5, :]
    lnx_g, lnx_b = v512_ref[5:6, :], v512_ref[6:7, :]
    log_w = -DECAY_SCALE * _sigmoid(w0 + lora_out[:, 0:C_WIDTH])
    iclr = _sigmoid(a0 + lora_out[:, C_WIDTH:2 * C_WIDTH])
    seg = seg_ref[...]
    kk = k * k_k
    kk = kk * lax.rsqrt(jnp.maximum(_head_sums(kk * kk, seg), 1e-12))
    k = k * (1.0 + (iclr - 1.0) * k_a)
    bonus = _head_sums(r * k * r_k, seg) * v
    log_hi, log_lo = _split(log_w)
    cum_all = _dot(tri_ref[...], log_hi) + _dot(tri_ref[...], log_lo)
    a_vec = -kk
    b_vec = kk * iclr

    row = lax.broadcasted_iota(jnp.int32, (t, 2 * t), 0)
    col = lax.broadcasted_iota(jnp.int32, (t, 2 * t), 1) % t
    strict = col < row
    incl = col <= row
    eye = (col == row).astype(F32)
    same_head = (lax.broadcasted_iota(jnp.int32, (LANES, LANES), 0) // HEAD_DIM
                 == lax.broadcasted_iota(jnp.int32, (LANES, LANES), 1) // HEAD_DIM)
    n_chunks = ts // t
    n_pairs = C_WIDTH // LANES
    combos = [(c, pr) for c in range(n_chunks) for pr in range(n_pairs)]

    def bf(arr):
        return arr.astype(BF16)

    def stk(arr):
        return _stack_heads(bf(arr))

    r_abs, a_abs, r_rel, a_rel, k_rel, b_rel, k_end, b_end, vals, e_end = ({} for _ in range(10))
    for c in range(n_chunks):
        rows = slice(c * t, (c + 1) * t)
        cum = cum_all[rows, :]
        mid = cum_all[c * t + t // 2 - 1:c * t + t // 2, :]
        end = cum_all[(c + 1) * t - 1:(c + 1) * t, :]
        e_mid = jnp.exp(-mid)
        e_end_mid = jnp.exp(end - mid)
        e_inv = jnp.exp(mid - cum)
        e_end[c] = jnp.exp(end)
        r_abs[c] = r[rows, :] * jnp.exp(cum)
        a_abs_c = a_vec[rows, :] * jnp.exp(cum - log_w[rows, :])
        k_rel_c = k[rows, :] * e_inv
        b_rel_c = b_vec[rows, :] * e_inv
        a_abs[c] = bf(a_abs_c)
        r_rel[c] = bf(r_abs[c] * e_mid)
        a_rel[c] = bf(a_abs_c * e_mid)
        k_rel[c] = bf(k_rel_c)
        b_rel[c] = bf(b_rel_c)
        k_end[c] = bf(k_rel_c * e_end_mid)
        b_end[c] = bf(b_rel_c * e_end_mid)
        vals[c] = bf(v[rows, :])

    def lanes(pr):
        return slice(pr * LANES, (pr + 1) * LANES)

    v_st, l_ak, a_rb, a_rk, inv, power = ({} for _ in range(6))
    for c, pr in combos:
        ln = lanes(pr)
        v_st[c, pr] = _stack_heads(vals[c][:, ln])
        scores = _dot_nt(
            jnp.concatenate([a_rel[c][:, ln], r_rel[c][:, ln]], axis=0),
            jnp.concatenate([_stack_heads(b_rel[c][:, ln]), _stack_heads(k_rel[c][:, ln])], axis=0))
        power[c, pr] = jnp.where(strict, scores[0:t, 0:2 * t], 0.0)
        l_ak[c, pr] = bf(jnp.where(strict, scores[0:t, 2 * t:4 * t], 0.0))
        a_rb[c, pr] = bf(jnp.where(incl, scores[t:2 * t, 0:2 * t], 0.0))
        a_rk[c, pr] = bf(jnp.where(incl, scores[t:2 * t, 2 * t:4 * t], 0.0))
        inv[c, pr] = eye + power[c, pr]
    for cp in combos:
        pb = bf(power[cp])
        power[cp] = _dot(pb, _stack_heads(pb))
    for _ in range(NEUMANN_FACTORS - 2):
        for cp in combos:
            both = _dot(bf(power[cp]), jnp.concatenate([stk(inv[cp]), stk(power[cp])], axis=1))
            inv[cp] = inv[cp] + both[:, 0:2 * t]
            power[cp] = both[:, 2 * t:4 * t]
    for cp in combos:
        inv[cp] = bf(inv[cp] + _dot(bf(power[cp]), stk(inv[cp])))

    lakv, w_loc, g_loc, y_loc, q_mat, phi, psi = ({} for _ in range(7))
    for cp in combos:
        lakv[cp] = _dot(l_ak[cp], v_st[cp])
    for c, pr in combos:
        both = _dot(inv[c, pr],
                    jnp.concatenate([stk(lakv[c, pr]), _stack_heads(a_abs[c][:, lanes(pr)])], axis=1))
        w_loc[c, pr] = bf(both[:, 0:LANES])
        g_loc[c, pr] = bf(both[:, LANES:2 * LANES])
    for c, pr in combos:
        cp = (c, pr)
        rhs = jnp.concatenate(
            [jnp.concatenate([_stack_heads(w_loc[cp]), _stack_heads(g_loc[cp])], axis=1),
             jnp.concatenate([v_st[cp], jnp.zeros_like(v_st[cp])], axis=1)], axis=0)
        both = _dot(jnp.concatenate([a_rb[cp], a_rk[cp]], axis=1), rhs)
        y_loc[cp] = both[:, 0:LANES]
        q_mat[cp] = bf(r_abs[c][:, lanes(pr)] + both[:, LANES:2 * LANES])
    for c, pr in combos:
        cp = (c, pr)
        ln = lanes(pr)
        phi[cp] = bf(jnp.where(same_head, _dot_tn(g_loc[cp], b_end[c][:, ln]), 0.0))
        psi[cp] = jnp.where(
            same_head,
            _dot_tn(jnp.concatenate([w_loc[cp], vals[c][:, ln]], axis=0),
                    jnp.concatenate([b_end[c][:, ln], k_end[c][:, ln]], axis=0)), 0.0)

    s_start = {}
    s_cur = [state[pr] for pr in range(n_pairs)]
    for c in range(n_chunks):
        for pr in range(n_pairs):
            s_bf = bf(s_cur[pr])
            s_start[c, pr] = s_bf
            s_cur[pr] = s_cur[pr] * e_end[c][:, lanes(pr)] + _dot(s_bf, phi[c, pr]) + psi[c, pr]
    for pr in range(n_pairs):
        state[pr] = s_cur[pr]
    y = jnp.concatenate(
        [jnp.concatenate([_dot_nt(q_mat[c, pr], s_start[c, pr]) + y_loc[c, pr]
                          for pr in range(n_pairs)], axis=1) for c in range(n_chunks)], axis=0)

    inv_n = 1.0 / HEAD_DIM
    yc = y - _head_sums(y, seg) * inv_n
    var = _head_sums(yc * yc, seg) * inv_n
    y = yc * lax.rsqrt(var + GN_EPS) * lnx_g + lnx_b + bonus
    out_c = (y * _silu(g_c)).astype(BF16)

    mix = (_dot(out_a, wout_ref[0:A_WIDTH, :])
           + _dot(out_b, wout_ref[A_WIDTH:A_WIDTH + B_WIDTH, :])
           + _dot(out_c, wout_ref[A_WIDTH + B_WIDTH:, :]))
    x1 = x + _rms_norm(mix, post_g)
    gate = _sigmoid(_dot(x1.astype(BF16), gatew_ref[...]) + gate_b)
    ple = _dot(p_ref[0, 0].astype(BF16), plew_ref[...])
    o_ref[0] = x1 + ple * gate


def _pad_rows(rows, width):
    arr = jnp.stack(rows).astype(F32)
    return jnp.pad(arr, ((0, SUBLANES - arr.shape[0]), (0, 0)))


def _layer_call(i, x, p, consts, weights):
    bsz, seq, _ = x.shape
    ts = min(SEQ_TILE, seq)
    assert seq % ts == 0 and ts % SGU_BLOCK == 0 and ts % SCAN_CHUNK == 0 and ts % CONV_ROWS == 0
    seg, tri = consts

    def full(arr):
        nd = arr.ndim
        return pl.BlockSpec(arr.shape, lambda b, s, _nd=nd: (0,) * _nd)

    ins = [x, p] + list(weights[:9]) + [seg, tri] + list(weights[9:])
    in_specs = [pl.BlockSpec((1, ts, D_MODEL), lambda b, s: (b, s, 0)),
                pl.BlockSpec((1, 1, ts, PLE_DIM), lambda b, s, _i=i: (_i, b, s, 0))]
    in_specs += [full(a) for a in ins[2:]]
    scratch = [
        pltpu.VMEM((ts + CONV_HIST, B_WIDTH), F32),
        pltpu.VMEM((ts + SUBLANES, C_SHIFT_COLS), F32),
        pltpu.VMEM((C_WIDTH // LANES, LANES, LANES), F32),
    ]
    return pl.pallas_call(
        _layer_kernel,
        grid=(bsz, seq // ts),
        in_specs=in_specs,
        out_specs=pl.BlockSpec((1, ts, D_MODEL), lambda b, s: (b, s, 0)),
        out_shape=jax.ShapeDtypeStruct(x.shape, F32),
        scratch_shapes=scratch,
        compiler_params=pltpu.CompilerParams(
            dimension_semantics=("arbitrary", "arbitrary"),
            vmem_limit_bytes=VMEM_LIMIT_BYTES),
        name=f"layer{i}",
    )(*ins)


def kernel(x, p, pre_norm_g, w_in, sgu_ln_g, sgu_ln_b, sgu_w, sgu_b, conv_w, conv_b, conv_ln_g,
           conv_ln_b, pw_w, pw_b, shift_mu, w0, w_up, a0, a_up, k_k, k_a, r_k, lnx_g, lnx_b,
           w_out, post_norm_g, ple_w, ple_gate_w, ple_gate_b):
    depth = w_in.shape[0]
    seq = x.shape[1]
    ts = min(SEQ_TILE, seq)
    head = jnp.arange(SEG_WIDTH) // HEAD_DIM
    seg = (head[:, None] == head[None, :]).astype(BF16)
    pos = jnp.arange(ts)
    tri = ((pos[:, None] // SCAN_CHUNK == pos[None, :] // SCAN_CHUNK)
           & (pos[None, :] <= pos[:, None])).astype(BF16)
    zeros = jnp.zeros((LORA, C_WIDTH), F32)
    for i in range(depth):
        lora = jnp.concatenate([jnp.concatenate([w_up[i], zeros], axis=1),
                                jnp.concatenate([zeros, a_up[i]], axis=1)], axis=0).astype(BF16)
        weights = (
            w_in[i].astype(BF16), w_out[i].astype(BF16), ple_gate_w[i].astype(BF16),
            ple_w[i].astype(BF16),
            sgu_w[i].transpose(1, 0, 2).reshape(SGU_BLOCK, -1),
            jnp.repeat(sgu_b[i].T, HEAD_DIM, axis=1),
            jnp.pad(conv_w[i], ((0, 1), (0, 0))),
            pw_w[i].astype(BF16), lora,
            _pad_rows([pre_norm_g[i], post_norm_g[i], ple_gate_b[i]], D_MODEL),
            _pad_rows([sgu_ln_g[i], sgu_ln_b[i], conv_b[i], conv_ln_g[i], conv_ln_b[i], pw_b[i]],
                      B_WIDTH),
            _pad_rows([w0[i], a0[i], k_k[i], k_a[i], r_k[i].reshape(-1), lnx_g[i], lnx_b[i]],
                      C_WIDTH),
            shift_mu[i].reshape(1, -1),
        )
        x = _layer_call(i, x, p, (seg, tri), weights)
    return x
```

```python
import functools
import math

import jax
import jax.numpy as jnp
from jax import lax
from jax.experimental import pallas as pl
from jax.experimental.pallas import tpu as pltpu

F32 = jnp.float32
BF16 = jnp.bfloat16

D_MODEL = 1024
A_WIDTH = 256
B_WIDTH = 256
C_WIDTH = 512
HEAD_DIM = 64
LORA = 64
CONV_WIDTH = 31
PLE_DIM = 256
SGU_BLOCK = 128
SGU_CHUNK = 64
OFF_A = 0
OFF_B = 3 * A_WIDTH
OFF_C = OFF_B + 3 * B_WIDTH
C_SHIFT_COLS = 3 * C_WIDTH + 2 * LORA
OFF_G = OFF_C + C_SHIFT_COLS
IN_COLS = OFF_G + C_WIDTH
RMS_EPS = 1e-6
LN_EPS = 1e-5
GN_EPS = 64e-5
DECAY_SCALE = math.exp(-0.5)

LANES = 128
SUBLANES = 8
SCAN_CHUNK = 64
NEUMANN_FACTORS = 6
assert SCAN_CHUNK == HEAD_DIM
SEQ_TILE = 256
CONV_HIST = 32
CONV_ROWS = 64
PROJ_COLS = 256
SEG_WIDTH = 256
VMEM_LIMIT_BYTES = 56 * 1024 * 1024


def _dot(a, b):
    return jnp.dot(a, b, preferred_element_type=F32)


def _dot_nt(a, b):
    return lax.dot_general(a, b, (((1,), (1,)), ((), ())), preferred_element_type=F32)


def _dot_tn(a, b):
    return lax.dot_general(a, b, (((0,), (0,)), ((), ())), preferred_element_type=F32)


def _split(x):
    hi = x.astype(BF16)
    lo = (x - hi.astype(F32)).astype(BF16)
    return hi, lo


def _head_sums(x, seg):
    width = seg.shape[0]
    return jnp.concatenate([_dot(x[:, lo:lo + width].astype(BF16), seg)
                            for lo in range(0, x.shape[1], width)], axis=1)


def _sigmoid(x):
    return jax.nn.sigmoid(x)


def _silu(x):
    return x * jax.nn.sigmoid(x)


def _layer_norm(x, g, b):
    mu = jnp.mean(x, axis=-1, keepdims=True)
    xc = x - mu
    var = jnp.mean(xc * xc, axis=-1, keepdims=True)
    return xc * lax.rsqrt(var + LN_EPS) * g + b


def _rms_norm(x, g):
    return x * lax.rsqrt(jnp.mean(x * x, axis=-1, keepdims=True) + RMS_EPS) * g


def _stack_heads(xp):
    lo = lax.broadcasted_iota(jnp.int32, xp.shape, 1) < HEAD_DIM
    zero = jnp.zeros_like(xp)
    return jnp.concatenate([jnp.where(lo, xp, zero), jnp.where(lo, zero, xp)], axis=0)


def _zero_after(x):
    bits = lax.bitcast_convert_type(x, jnp.uint32)
    bits = lax.shift_right_logical(lax.shift_right_logical(bits, jnp.uint32(16)), jnp.uint32(16))
    return lax.bitcast_convert_type(bits, F32)


def _conv_group(ybuf, convw_ref, base, res, dep):
    first = CONV_HIST - (CONV_WIDTH - 1)
    extra = SUBLANES if res else 0
    part = None
    for j in range(CONV_WIDTH):
        if (first + j) % SUBLANES != res:
            continue
        term = (convw_ref[j:j + 1, :] + dep) * ybuf[pl.ds(base + first + j - res, CONV_ROWS + extra), :]
        part = term if part is None else part + term
    return None if part is None else part[res:res + CONV_ROWS, :]


def _layer_kernel(x_ref, p_ref, win_ref, wout_ref, gatew_ref, plew_ref, sguw_ref, sgub_ref,
                  convw_ref, pww_ref, lora_ref, seg_ref, tri_ref, v1024_ref, v256_ref,
                  v512_ref, mu_ref, o_ref,
                  ybuf, zbuf, state):
    ts = x_ref.shape[1]
    t = SCAN_CHUNK

    @pl.when(pl.program_id(1) == 0)
    def _reset():
        ybuf[0:CONV_HIST, :] = jnp.zeros((CONV_HIST, B_WIDTH), F32)
        zbuf[0:SUBLANES, :] = jnp.zeros((SUBLANES, C_SHIFT_COLS), F32)
        state[...] = jnp.zeros(state.shape, F32)

    x = x_ref[0]
    pre_g = v1024_ref[0:1, :]
    post_g = v1024_ref[1:2, :]
    gate_b = v1024_ref[2:3, :]
    hb = _rms_norm(x, pre_g).astype(BF16)

    z_b = _dot(hb, win_ref[:, OFF_B:OFF_C])
    g_b = z_b[:, 2 * B_WIDTH:3 * B_WIDTH]
    ybuf[CONV_HIST:CONV_HIST + ts, :] = z_b[:, 0:B_WIDTH] * _sigmoid(z_b[:, B_WIDTH:2 * B_WIDTH])
    n_rt = ts // CONV_ROWS
    conv_acc = [None] * n_rt
    groups = [(rt, res) for rt in range(n_rt) for res in range(SUBLANES)]
    half = ts // 2
    pieces = [(rlo, clo) for clo in range(OFF_C, IN_COLS, PROJ_COLS) for rlo in (0, half)]
    per = -(-len(groups) // len(pieces))
    gc_parts = {}
    gi = 0
    for rlo, clo in pieces:
        chi = min(clo + PROJ_COLS, IN_COLS)
        res_piece = _dot(hb[rlo:rlo + half, :], win_ref[:, clo:chi])
        zhi = min(chi, OFF_G)
        if clo < OFF_G:
            zbuf[SUBLANES + rlo:SUBLANES + rlo + half, clo - OFF_C:zhi - OFF_C] = res_piece[:, 0:zhi - clo]
        if chi > OFF_G:
            gc_parts[rlo, clo] = res_piece[:, max(OFF_G - clo, 0):]
        dep = _zero_after(res_piece[0:1, 0:B_WIDTH])
        for _ in range(per):
            if gi < len(groups):
                rt, res = groups[gi]
                gi += 1
                part = _conv_group(ybuf, convw_ref, rt * CONV_ROWS, res, dep)
                if part is not None:
                    conv_acc[rt] = part if conv_acc[rt] is None else conv_acc[rt] + part
    assert gi == len(groups)
    conv_rows = [acc + v256_ref[2:3, :] for acc in conv_acc]
    g_c = jnp.concatenate(
        [jnp.concatenate([gc_parts[rlo, clo] for clo in sorted({c for _, c in gc_parts})], axis=1)
         for rlo in (0, half)], axis=0)
    ybuf[0:CONV_HIST, :] = ybuf[ts:ts + CONV_HIST, :]
    yc = _silu(_layer_norm(jnp.concatenate(conv_rows, axis=0), v256_ref[3:4, :], v256_ref[4:5, :]))
    out_b = ((_dot(yc.astype(BF16), pww_ref[...]) + v256_ref[5:6, :]) * _silu(g_b)).astype(BF16)

    zc = zbuf[SUBLANES:SUBLANES + ts, :]
    zs = zc + mu_ref[...] * (zbuf[SUBLANES - 1:SUBLANES - 1 + ts, :] - zc)
    zbuf[0:SUBLANES, :] = zbuf[ts:ts + SUBLANES, :]

    r = zs[:, 0:C_WIDTH]
    k = zs[:, C_WIDTH:2 * C_WIDTH]
    v = zs[:, 2 * C_WIDTH:3 * C_WIDTH]
    lora_in = zs[:, 3 * C_WIDTH:C_SHIFT_COLS]
    is_decay = lax.broadcasted_iota(jnp.int32, lora_in.shape, 1) < LORA
    lora_in = jnp.where(is_decay, jnp.tanh(lora_in), lora_in).astype(BF16)
    lora_out = _dot(lora_in, lora_ref[...])
    w0, a0 = v512_ref[0:1, :], v512_ref[1:2, :]
    k_k, k_a, r_k = v512_ref[2:3, :], v512_ref[3:4, :], v512_ref[4:5, :]
    lnx_g, lnx_b = v512_ref[5:6, :], v512_ref[6:7, :]
    log_w = -DECAY_SCALE * _sigmoid(w0 + lora_out[:, 0:C_WIDTH])
    iclr = _sigmoid(a0 + lora_out[:, C_WIDTH:2 * C_WIDTH])
    seg = seg_ref[...]
    kk = k * k_k
    kk = kk * lax.rsqrt(jnp.maximum(_head_sums(kk * kk, seg), 1e-12))
    k = k * (1.0 + (iclr - 1.0) * k_a)
    bonus = _head_sums(r * k * r_k, seg) * v
    log_hi, log_lo = _split(log_w)
    cum_all = _dot(tri_ref[...], log_hi) + _dot(tri_ref[...], log_lo)
    a_vec = -kk
    b_vec = kk * iclr

    row = lax.broadcasted_iota(jnp.int32, (t, 2 * t), 0)
    col = lax.broadcasted_iota(jnp.int32, (t, 2 * t), 1) % t
    strict = col < row
    incl = col <= row
    eye = (col == row).astype(F32)
    same_head = (lax.broadcasted_iota(jnp.int32, (LANES, LANES), 0) // HEAD_DIM
                 == lax.broadcasted_iota(jnp.int32, (LANES, LANES), 1) // HEAD_DIM)
    n_chunks = ts // t
    n_pairs = C_WIDTH // LANES
    combos = [(c, pr) for c in range(n_chunks) for pr in range(n_pairs)]

    def bf(arr):
        return arr.astype(BF16)

    def stk(arr):
        return _stack_heads(bf(arr))

    r_abs, a_abs, r_rel, a_rel, k_rel, b_rel, k_end, b_end, vals, e_end = ({} for _ in range(10))
    for c in range(n_chunks):
        rows = slice(c * t, (c + 1) * t)
        cum = cum_all[rows, :]
        mid = cum_all[c * t + t // 2 - 1:c * t + t // 2, :]
        end = cum_all[(c + 1) * t - 1:(c + 1) * t, :]
        e_mid = jnp.exp(-mid)
        e_end_mid = jnp.exp(end - mid)
        e_inv = jnp.exp(mid - cum)
        e_end[c] = jnp.exp(end)
        r_abs[c] = r[rows, :] * jnp.exp(cum)
        a_abs_c = a_vec[rows, :] * jnp.exp(cum - log_w[rows, :])
        k_rel_c = k[rows, :] * e_inv
        b_rel_c = b_vec[rows, :] * e_inv
        a_abs[c] = bf(a_abs_c)
        r_rel[c] = bf(r_abs[c] * e_mid)
        a_rel[c] = bf(a_abs_c * e_mid)
        k_rel[c] = bf(k_rel_c)
        b_rel[c] = bf(b_rel_c)
        k_end[c] = bf(k_rel_c * e_end_mid)
        b_end[c] = bf(b_rel_c * e_end_mid)
        vals[c] = bf(v[rows, :])

    z_parts = []
    for piece in range(3):
        dep = _zero_after(r_abs[min(piece + 1, n_chunks - 1)][0:1, 0:LANES]).astype(BF16)
        lhs = jnp.concatenate([hb[:, 0:LANES] + dep, hb[:, LANES:]], axis=1)
        z_parts.append(_dot(lhs, win_ref[:, OFF_A + piece * A_WIDTH:OFF_A + (piece + 1) * A_WIDTH]))
    u_a = z_parts[0]
    v_a = _layer_norm(z_parts[1], v256_ref[0:1, :], v256_ref[1:2, :])
    g_a = z_parts[2]
    wi = lax.broadcasted_iota(jnp.int32, sguw_ref.shape, 0) // SGU_CHUNK
    wj = (lax.broadcasted_iota(jnp.int32, sguw_ref.shape, 1) % SGU_BLOCK) // SGU_CHUNK
    w_sgu = jnp.where(wj <= wi, sguw_ref[...], 0.0).astype(BF16)
    mixed_rows = []
    for nb in range(ts // SGU_BLOCK):
        vb = v_a[nb * SGU_BLOCK:(nb + 1) * SGU_BLOCK, :]
        pairs = []
        for pr in range(A_WIDTH // LANES):
            st = _stack_heads(vb[:, pr * LANES:(pr + 1) * LANES]).astype(BF16)
            pairs.append(_dot(w_sgu[:, 2 * pr * SGU_BLOCK:(2 * pr + 2) * SGU_BLOCK], st))
        mixed_rows.append(jnp.concatenate(pairs, axis=1) + sgub_ref[...])
    mixed = jnp.concatenate(mixed_rows, axis=0)
    out_a = (u_a * mixed * _silu(g_a)).astype(BF16)

    def lanes(pr):
        return slice(pr * LANES, (pr + 1) * LANES)

    v_st, l_ak, a_rb, a_rk, inv, power = ({} for _ in range(6))
    for c, pr in combos:
        ln = lanes(pr)
        v_st[c, pr] = _stack_heads(vals[c][:, ln])
        scores = _dot_nt(
            jnp.concatenate([a_rel[c][:, ln], r_rel[c][:, ln]], axis=0),
            jnp.concatenate([_stack_heads(b_rel[c][:, ln]), _stack_heads(k_rel[c][:, ln])], axis=0))
        power[c, pr] = jnp.where(strict, scores[0:t, 0:2 * t], 0.0)
        l_ak[c, pr] = bf(jnp.where(strict, scores[0:t, 2 * t:4 * t], 0.0))
        a_rb[c, pr] = bf(jnp.where(incl, scores[t:2 * t, 0:2 * t], 0.0))
        a_rk[c, pr] = bf(jnp.where(incl, scores[t:2 * t, 2 * t:4 * t], 0.0))
        inv[c, pr] = eye + power[c, pr]
    for cp in combos:
        pb = bf(power[cp])
        power[cp] = _dot(pb, _stack_heads(pb))
    for _ in range(NEUMANN_FACTORS - 2):
        for cp in combos:
            both = _dot(bf(power[cp]), jnp.concatenate([stk(inv[cp]), stk(power[cp])], axis=1))
            inv[cp] = inv[cp] + both[:, 0:2 * t]
            power[cp] = both[:, 2 * t:4 * t]
    for cp in combos:
        inv[cp] = bf(inv[cp] + _dot(bf(power[cp]), stk(inv[cp])))

    lakv, w_loc, g_loc, y_loc, q_mat, phi, psi = ({} for _ in range(7))
    for cp in combos:
        lakv[cp] = _dot(l_ak[cp], v_st[cp])
    for c, pr in combos:
        both = _dot(inv[c, pr],
                    jnp.concatenate([stk(lakv[c, pr]), _stack_heads(a_abs[c][:, lanes(pr)])], axis=1))
        w_loc[c, pr] = bf(both[:, 0:LANES])
        g_loc[c, pr] = bf(both[:, LANES:2 * LANES])
    for c, pr in combos:
        cp = (c, pr)
        rhs = jnp.concatenate(
            [jnp.concatenate([_stack_heads(w_loc[cp]), _stack_heads(g_loc[cp])], axis=1),
             jnp.concatenate([v_st[cp], jnp.zeros_like(v_st[cp])], axis=1)], axis=0)
        both = _dot(jnp.concatenate([a_rb[cp], a_rk[cp]], axis=1), rhs)
        y_loc[cp] = both[:, 0:LANES]
        q_mat[cp] = bf(r_abs[c][:, lanes(pr)] + both[:, LANES:2 * LANES])
    for c, pr in combos:
        cp = (c, pr)
        ln = lanes(pr)
        phi[cp] = bf(jnp.where(same_head, _dot_tn(g_loc[cp], b_end[c][:, ln]), 0.0))
        psi[cp] = jnp.where(
            same_head,
            _dot_tn(jnp.concatenate([w_loc[cp], vals[c][:, ln]], axis=0),
                    jnp.concatenate([b_end[c][:, ln], k_end[c][:, ln]], axis=0)), 0.0)

    s_start = {}
    s_cur = [state[pr] for pr in range(n_pairs)]
    for c in range(n_chunks):
        for pr in range(n_pairs):
            s_bf = bf(s_cur[pr])
            s_start[c, pr] = s_bf
            s_cur[pr] = s_cur[pr] * e_end[c][:, lanes(pr)] + _dot(s_bf, phi[c, pr]) + psi[c, pr]
    for pr in range(n_pairs):
        state[pr] = s_cur[pr]
    y = jnp.concatenate(
        [jnp.concatenate([_dot_nt(q_mat[c, pr], s_start[c, pr]) + y_loc[c, pr]
                          for pr in range(n_pairs)], axis=1) for c in range(n_chunks)], axis=0)

    inv_n = 1.0 / HEAD_DIM
    yc = y - _head_sums(y, seg) * inv_n
    var = _head_sums(yc * yc, seg) * inv_n
    y = yc * lax.rsqrt(var + GN_EPS) * lnx_g + lnx_b + bonus
    out_c = (y * _silu(g_c)).astype(BF16)

    mix = (_dot(out_a, wout_ref[0:A_WIDTH, :])
           + _dot(out_b, wout_ref[A_WIDTH:A_WIDTH + B_WIDTH, :])
           + _dot(out_c, wout_ref[A_WIDTH + B_WIDTH:, :]))
    x1 = x + _rms_norm(mix, post_g)
    gate = _sigmoid(_dot(x1.astype(BF16), gatew_ref[...]) + gate_b)
    ple = _dot(p_ref[0, 0].astype(BF16), plew_ref[...])
    o_ref[0] = x1 + ple * gate


def _pad_rows(rows, width):
    arr = jnp.stack(rows).astype(F32)
    return jnp.pad(arr, ((0, SUBLANES - arr.shape[0]), (0, 0)))


def _layer_call(i, x, p, consts, weights):
    bsz, seq, _ = x.shape
    ts = min(SEQ_TILE, seq)
    assert seq % ts == 0 and ts % SGU_BLOCK == 0 and ts % SCAN_CHUNK == 0 and ts % CONV_ROWS == 0
    seg, tri = consts

    def full(arr):
        nd = arr.ndim
        return pl.BlockSpec(arr.shape, lambda b, s, _nd=nd: (0,) * _nd)

    ins = [x, p] + list(weights[:9]) + [seg, tri] + list(weights[9:])
    in_specs = [pl.BlockSpec((1, ts, D_MODEL), lambda b, s: (b, s, 0)),
                pl.BlockSpec((1, 1, ts, PLE_DIM), lambda b, s, _i=i: (_i, b, s, 0))]
    in_specs += [full(a) for a in ins[2:]]
    scratch = [
        pltpu.VMEM((ts + CONV_HIST, B_WIDTH), F32),
        pltpu.VMEM((ts + SUBLANES, C_SHIFT_COLS), F32),
        pltpu.VMEM((C_WIDTH // LANES, LANES, LANES), F32),
    ]
    return pl.pallas_call(
        _layer_kernel,
        grid=(bsz, seq // ts),
        in_specs=in_specs,
        out_specs=pl.BlockSpec((1, ts, D_MODEL), lambda b, s: (b, s, 0)),
        out_shape=jax.ShapeDtypeStruct(x.shape, F32),
        scratch_shapes=scratch,
        compiler_params=pltpu.CompilerParams(
            dimension_semantics=("arbitrary", "arbitrary"),
            vmem_limit_bytes=VMEM_LIMIT_BYTES),
        name=f"layer{i}",
    )(*ins)


def kernel(x, p, pre_norm_g, w_in, sgu_ln_g, sgu_ln_b, sgu_w, sgu_b, conv_w, conv_b, conv_ln_g,
           conv_ln_b, pw_w, pw_b, shift_mu, w0, w_up, a0, a_up, k_k, k_a, r_k, lnx_g, lnx_b,
           w_out, post_norm_g, ple_w, ple_gate_w, ple_gate_b):
    depth = w_in.shape[0]
    seq = x.shape[1]
    ts = min(SEQ_TILE, seq)
    head = jnp.arange(SEG_WIDTH) // HEAD_DIM
    seg = (head[:, None] == head[None, :]).astype(BF16)
    pos = jnp.arange(ts)
    tri = ((pos[:, None] // SCAN_CHUNK == pos[None, :] // SCAN_CHUNK)
           & (pos[None, :] <= pos[:, None])).astype(BF16)
    zeros = jnp.zeros((LORA, C_WIDTH), F32)
    for i in range(depth):
        lora = jnp.concatenate([jnp.concatenate([w_up[i], zeros], axis=1),
                                jnp.concatenate([zeros, a_up[i]], axis=1)], axis=0).astype(BF16)
        weights = (
            w_in[i].astype(BF16), w_out[i].astype(BF16), ple_gate_w[i].astype(BF16),
            ple_w[i].astype(BF16),
            sgu_w[i].transpose(1, 0, 2).reshape(SGU_BLOCK, -1),
            jnp.repeat(sgu_b[i].T, HEAD_DIM, axis=1),
            jnp.pad(conv_w[i], ((0, 1), (0, 0))),
            pw_w[i].astype(BF16), lora,
            _pad_rows([pre_norm_g[i], post_norm_g[i], ple_gate_b[i]], D_MODEL),
            _pad_rows([sgu_ln_g[i], sgu_ln_b[i], conv_b[i], conv_ln_g[i], conv_ln_b[i], pw_b[i]],
                      B_WIDTH),
            _pad_rows([w0[i], a0[i], k_k[i], k_a[i], r_k[i].reshape(-1), lnx_g[i], lnx_b[i]],
                      C_WIDTH),
            shift_mu[i].reshape(1, -1),
        )
        x = _layer_call(i, x, p, (seg, tri), weights)
    return x
```

```python
import functools
import math

import jax
import jax.numpy as jnp
from jax import lax
from jax.experimental import pallas as pl
from jax.experimental.pallas import tpu as pltpu

F32 = jnp.float32
BF16 = jnp.bfloat16

D_MODEL = 1024
A_WIDTH = 256
B_WIDTH = 256
C_WIDTH = 512
HEAD_DIM = 64
LORA = 64
CONV_WIDTH = 31
PLE_DIM = 256
SGU_BLOCK = 128
SGU_CHUNK = 64
OFF_A = 0
OFF_B = 3 * A_WIDTH
OFF_C = OFF_B + 3 * B_WIDTH
C_SHIFT_COLS = 3 * C_WIDTH + 2 * LORA
OFF_G = OFF_C + C_SHIFT_COLS
IN_COLS = OFF_G + C_WIDTH
RMS_EPS = 1e-6
LN_EPS = 1e-5
GN_EPS = 64e-5
DECAY_SCALE = math.exp(-0.5)

LANES = 128
SUBLANES = 8
SCAN_CHUNK = 64
NEUMANN_FACTORS = 6
assert SCAN_CHUNK == HEAD_DIM
SEQ_TILE = 256
CONV_HIST = 32
CONV_ROWS = 64
PROJ_COLS = 256
SEG_WIDTH = 256
VMEM_LIMIT_BYTES = 56 * 1024 * 1024


def _dot(a, b):
    return jnp.dot(a, b, preferred_element_type=F32)


def _dot_nt(a, b):
    return lax.dot_general(a, b, (((1,), (1,)), ((), ())), preferred_element_type=F32)


def _dot_tn(a, b):
    return lax.dot_general(a, b, (((0,), (0,)), ((), ())), preferred_element_type=F32)


def _split(x):
    hi = x.astype(BF16)
    lo = (x - hi.astype(F32)).astype(BF16)
    return hi, lo


def _head_sums(x, seg):
    width = seg.shape[0]
    return jnp.concatenate([_dot(x[:, lo:lo + width].astype(BF16), seg)
                            for lo in range(0, x.shape[1], width)], axis=1)


def _sigmoid(x):
    return jax.nn.sigmoid(x)


def _silu(x):
    return x * jax.nn.sigmoid(x)


def _layer_norm(x, g, b):
    mu = jnp.mean(x, axis=-1, keepdims=True)
    xc = x - mu
    var = jnp.mean(xc * xc, axis=-1, keepdims=True)
    return xc * lax.rsqrt(var + LN_EPS) * g + b


def _rms_norm(x, g):
    return x * lax.rsqrt(jnp.mean(x * x, axis=-1, keepdims=True) + RMS_EPS) * g


def _stack_heads(xp):
    lo = lax.broadcasted_iota(jnp.int32, xp.shape, 1) < HEAD_DIM
    zero = jnp.zeros_like(xp)
    return jnp.concatenate([jnp.where(lo, xp, zero), jnp.where(lo, zero, xp)], axis=0)


def _zero_after(x):
    bits = lax.bitcast_convert_type(x, jnp.uint32)
    bits = lax.shift_right_logical(lax.shift_right_logical(bits, jnp.uint32(16)), jnp.uint32(16))
    return lax.bitcast_convert_type(bits, F32)


def _conv_group(ybuf, convw_ref, base, res, dep):
    first = CONV_HIST - (CONV_WIDTH - 1)
    extra = SUBLANES if res else 0
    part = None
    for j in range(CONV_WIDTH):
        if (first + j) % SUBLANES != res:
            continue
        term = (convw_ref[j:j + 1, :] + dep) * ybuf[pl.ds(base + first + j - res, CONV_ROWS + extra), :]
        part = term if part is None else part + term
    return None if part is None else part[res:res + CONV_ROWS, :]


def _layer_kernel(x_ref, p_ref, win_ref, wout_ref, gatew_ref, plew_ref, sguw_ref, sgub_ref,
                  convw_ref, pww_ref, lora_ref, seg_ref, tri_ref, v1024_ref, v256_ref,
                  v512_ref, mu_ref, o_ref,
                  ybuf, zbuf, state):
    ts = x_ref.shape[1]
    t = SCAN_CHUNK

    @pl.when(pl.program_id(1) == 0)
    def _reset():
        ybuf[0:CONV_HIST, :] = jnp.zeros((CONV_HIST, B_WIDTH), F32)
        zbuf[0:SUBLANES, :] = jnp.zeros((SUBLANES, C_SHIFT_COLS), F32)
        state[...] = jnp.zeros(state.shape, F32)

    x = x_ref[0]
    pre_g = v1024_ref[0:1, :]
    post_g = v1024_ref[1:2, :]
    gate_b = v1024_ref[2:3, :]
    hb = _rms_norm(x, pre_g).astype(BF16)

    z_b = _dot(hb, win_ref[:, OFF_B:OFF_C])
    g_b = z_b[:, 2 * B_WIDTH:3 * B_WIDTH]
    ybuf[CONV_HIST:CONV_HIST + ts, :] = z_b[:, 0:B_WIDTH] * _sigmoid(z_b[:, B_WIDTH:2 * B_WIDTH])
    n_rt = ts // CONV_ROWS
    conv_acc = [None] * n_rt
    groups = [(rt, res) for rt in range(n_rt) for res in range(SUBLANES)]
    half = ts // 2
    pieces = [(rlo, clo) for clo in range(OFF_C, IN_COLS, PROJ_COLS) for rlo in (0, half)]
    per = -(-len(groups) // len(pieces))
    gc_parts = {}
    gi = 0
    for rlo, clo in pieces:
        chi = min(clo + PROJ_COLS, IN_COLS)
        res_piece = _dot(hb[rlo:rlo + half, :], win_ref[:, clo:chi])
        zhi = min(chi, OFF_G)
        if clo < OFF_G:
            zbuf[SUBLANES + rlo:SUBLANES + rlo + half, clo - OFF_C:zhi - OFF_C] = res_piece[:, 0:zhi - clo]
        if chi > OFF_G:
            gc_parts[rlo, clo] = res_piece[:, max(OFF_G - clo, 0):]
        dep = _zero_after(res_piece[0:1, 0:B_WIDTH])
        for _ in range(per):
            if gi < len(groups):
                rt, res = groups[gi]
                gi += 1
                part = _conv_group(ybuf, convw_ref, rt * CONV_ROWS, res, dep)
                if part is not None:
                    conv_acc[rt] = part if conv_acc[rt] is None else conv_acc[rt] + part
    assert gi == len(groups)
    conv_rows = [acc + v256_ref[2:3, :] for acc in conv_acc]
    g_c = jnp.concatenate(
        [jnp.concatenate([gc_parts[rlo, clo] for clo in sorted({c for _, c in gc_parts})], axis=1)
         for rlo in (0, half)], axis=0)
    ybuf[0:CONV_HIST, :] = ybuf[ts:ts + CONV_HIST, :]
    yc = _silu(_layer_norm(jnp.concatenate(conv_rows, axis=0), v256_ref[3:4, :], v256_ref[4:5, :]))
    out_b = ((_dot(yc.astype(BF16), pww_ref[...]) + v256_ref[5:6, :]) * _silu(g_b)).astype(BF16)

    zc = zbuf[SUBLANES:SUBLANES + ts, :]
    first_row = lax.broadcasted_iota(jnp.int32, zc.shape, 0) == 0
    prev = jnp.where(first_row, zbuf[SUBLANES - 1:SUBLANES, :], pltpu.roll(zc, 1, axis=0))
    zs = zc + mu_ref[...] * (prev - zc)
    zbuf[0:SUBLANES, :] = zbuf[ts:ts + SUBLANES, :]

    r = zs[:, 0:C_WIDTH]
    k = zs[:, C_WIDTH:2 * C_WIDTH]
    v = zs[:, 2 * C_WIDTH:3 * C_WIDTH]
    lora_in = zs[:, 3 * C_WIDTH:C_SHIFT_COLS]
    is_decay = lax.broadcasted_iota(jnp.int32, lora_in.shape, 1) < LORA
    lora_in = jnp.where(is_decay, jnp.tanh(lora_in), lora_in).astype(BF16)
    lora_out = _dot(lora_in, lora_ref[...])
    w0, a0 = v512_ref[0:1, :], v512_ref[1:2, :]
    k_k, k_a, r_k = v512_ref[2:3, :], v512_ref[3:4, :], v512_ref[4:5, :]
    lnx_g, lnx_b = v512_ref[5:6, :], v512_ref[6:7, :]
    log_w = -DECAY_SCALE * _sigmoid(w0 + lora_out[:, 0:C_WIDTH])
    iclr = _sigmoid(a0 + lora_out[:, C_WIDTH:2 * C_WIDTH])
    seg = seg_ref[...]
    kk = k * k_k
    kk = kk * lax.rsqrt(jnp.maximum(_head_sums(kk * kk, seg), 1e-12))
    k = k * (1.0 + (iclr - 1.0) * k_a)
    bonus = _head_sums(r * k * r_k, seg) * v
    log_hi, log_lo = _split(log_w)
    cum_all = _dot(tri_ref[...], log_hi) + _dot(tri_ref[...], log_lo)
    a_vec = -kk
    b_vec = kk * iclr

    row = lax.broadcasted_iota(jnp.int32, (t, 2 * t), 0)
    col = lax.broadcasted_iota(jnp.int32, (t, 2 * t), 1) % t
    strict = col < row
    incl = col <= row
    eye = (col == row).astype(F32)
    same_head = (lax.broadcasted_iota(jnp.int32, (LANES, LANES), 0) // HEAD_DIM
                 == lax.broadcasted_iota(jnp.int32, (LANES, LANES), 1) // HEAD_DIM)
    n_chunks = ts // t
    n_pairs = C_WIDTH // LANES
    combos = [(c, pr) for c in range(n_chunks) for pr in range(n_pairs)]

    def bf(arr):
        return arr.astype(BF16)

    def stk(arr):
        return _stack_heads(bf(arr))

    r_abs, a_abs, r_rel, a_rel, k_rel, b_rel, k_end, b_end, vals, e_end = ({} for _ in range(10))
    for c in range(n_chunks):
        rows = slice(c * t, (c + 1) * t)
        cum = cum_all[rows, :]
        mid = cum_all[c * t + t // 2 - 1:c * t + t // 2, :]
        end = cum_all[(c + 1) * t - 1:(c + 1) * t, :]
        e_mid = jnp.exp(-mid)
        e_end_mid = jnp.exp(end - mid)
        e_inv = jnp.exp(mid - cum)
        e_end[c] = jnp.exp(end)
        r_abs[c] = r[rows, :] * jnp.exp(cum)
        a_abs_c = a_vec[rows, :] * jnp.exp(cum - log_w[rows, :])
        k_rel_c = k[rows, :] * e_inv
        b_rel_c = b_vec[rows, :] * e_inv
        a_abs[c] = bf(a_abs_c)
        r_rel[c] = bf(r_abs[c] * e_mid)
        a_rel[c] = bf(a_abs_c * e_mid)
        k_rel[c] = bf(k_rel_c)
        b_rel[c] = bf(b_rel_c)
        k_end[c] = bf(k_rel_c * e_end_mid)
        b_end[c] = bf(b_rel_c * e_end_mid)
        vals[c] = bf(v[rows, :])

    z_parts = []
    for piece in range(3):
        dep = _zero_after(r_abs[min(piece + 1, n_chunks - 1)][0:1, 0:LANES]).astype(BF16)
        lhs = jnp.concatenate([hb[:, 0:LANES] + dep, hb[:, LANES:]], axis=1)
        z_parts.append(_dot(lhs, win_ref[:, OFF_A + piece * A_WIDTH:OFF_A + (piece + 1) * A_WIDTH]))
    u_a = z_parts[0]
    v_a = _layer_norm(z_parts[1], v256_ref[0:1, :], v256_ref[1:2, :])
    g_a = z_parts[2]
    wi = lax.broadcasted_iota(jnp.int32, sguw_ref.shape, 0) // SGU_CHUNK
    wj = (lax.broadcasted_iota(jnp.int32, sguw_ref.shape, 1) % SGU_BLOCK) // SGU_CHUNK
    w_sgu = jnp.where(wj <= wi, sguw_ref[...], 0.0).astype(BF16)
    mixed_rows = []
    for nb in range(ts // SGU_BLOCK):
        vb = v_a[nb * SGU_BLOCK:(nb + 1) * SGU_BLOCK, :]
        pairs = []
        for pr in range(A_WIDTH // LANES):
            st = _stack_heads(vb[:, pr * LANES:(pr + 1) * LANES]).astype(BF16)
            pairs.append(_dot(w_sgu[:, 2 * pr * SGU_BLOCK:(2 * pr + 2) * SGU_BLOCK], st))
        mixed_rows.append(jnp.concatenate(pairs, axis=1) + sgub_ref[...])
    mixed = jnp.concatenate(mixed_rows, axis=0)
    out_a = (u_a * mixed * _silu(g_a)).astype(BF16)

    def lanes(pr):
        return slice(pr * LANES, (pr + 1) * LANES)

    v_st, l_ak, a_rb, a_rk, inv, power = ({} for _ in range(6))
    for c, pr in combos:
        ln = lanes(pr)
        v_st[c, pr] = _stack_heads(vals[c][:, ln])
        scores = _dot_nt(
            jnp.concatenate([a_rel[c][:, ln], r_rel[c][:, ln]], axis=0),
            jnp.concatenate([_stack_heads(b_rel[c][:, ln]), _stack_heads(k_rel[c][:, ln])], axis=0))
        power[c, pr] = jnp.where(strict, scores[0:t, 0:2 * t], 0.0)
        l_ak[c, pr] = bf(jnp.where(strict, scores[0:t, 2 * t:4 * t], 0.0))
        a_rb[c, pr] = bf(jnp.where(incl, scores[t:2 * t, 0:2 * t], 0.0))
        a_rk[c, pr] = bf(jnp.where(incl, scores[t:2 * t, 2 * t:4 * t], 0.0))
        inv[c, pr] = eye + power[c, pr]
    for cp in combos:
        pb = bf(power[cp])
        power[cp] = _dot(pb, _stack_heads(pb))
    for _ in range(NEUMANN_FACTORS - 2):
        for cp in combos:
            both = _dot(bf(power[cp]), jnp.concatenate([stk(inv[cp]), stk(power[cp])], axis=1))
            inv[cp] = inv[cp] + both[:, 0:2 * t]
            power[cp] = both[:, 2 * t:4 * t]
    for cp in combos:
        inv[cp] = bf(inv[cp] + _dot(bf(power[cp]), stk(inv[cp])))

    lakv, w_loc, g_loc, y_loc, q_mat, phi, psi = ({} for _ in range(7))
    for cp in combos:
        lakv[cp] = _dot(l_ak[cp], v_st[cp])
    for c, pr in combos:
        both = _dot(inv[c, pr],
                    jnp.concatenate([stk(lakv[c, pr]), _stack_heads(a_abs[c][:, lanes(pr)])], axis=1))
        w_loc[c, pr] = bf(both[:, 0:LANES])
        g_loc[c, pr] = bf(both[:, LANES:2 * LANES])
    for c, pr in combos:
        cp = (c, pr)
        rhs = jnp.concatenate(
            [jnp.concatenate([_stack_heads(w_loc[cp]), _stack_heads(g_loc[cp])], axis=1),
             jnp.concatenate([v_st[cp], jnp.zeros_like(v_st[cp])], axis=1)], axis=0)
        both = _dot(jnp.concatenate([a_rb[cp], a_rk[cp]], axis=1), rhs)
        y_loc[cp] = both[:, 0:LANES]
        q_mat[cp] = bf(r_abs[c][:, lanes(pr)] + both[:, LANES:2 * LANES])
    for c, pr in combos:
        cp = (c, pr)
        ln = lanes(pr)
        phi[cp] = bf(jnp.where(same_head, _dot_tn(g_loc[cp], b_end[c][:, ln]), 0.0))
        psi[cp] = jnp.where(
            same_head,
            _dot_tn(jnp.concatenate([w_loc[cp], vals[c][:, ln]], axis=0),
                    jnp.concatenate([b_end[c][:, ln], k_end[c][:, ln]], axis=0)), 0.0)

    s_start = {}
    s_cur = [state[pr] for pr in range(n_pairs)]
    for c in range(n_chunks):
        for pr in range(n_pairs):
            s_bf = bf(s_cur[pr])
            s_start[c, pr] = s_bf
            s_cur[pr] = s_cur[pr] * e_end[c][:, lanes(pr)] + _dot(s_bf, phi[c, pr]) + psi[c, pr]
    for pr in range(n_pairs):
        state[pr] = s_cur[pr]
    y = jnp.concatenate(
        [jnp.concatenate([_dot_nt(q_mat[c, pr], s_start[c, pr]) + y_loc[c, pr]
                          for pr in range(n_pairs)], axis=1) for c in range(n_chunks)], axis=0)

    inv_n = 1.0 / HEAD_DIM
    yc = y - _head_sums(y, seg) * inv_n
    var = _head_sums(yc * yc, seg) * inv_n
    y = yc * lax.rsqrt(var + GN_EPS) * lnx_g + lnx_b + bonus
    out_c = (y * _silu(g_c)).astype(BF16)

    mix = (_dot(out_a, wout_ref[0:A_WIDTH, :])
           + _dot(out_b, wout_ref[A_WIDTH:A_WIDTH + B_WIDTH, :])
           + _dot(out_c, wout_ref[A_WIDTH + B_WIDTH:, :]))
    x1 = x + _rms_norm(mix, post_g)
    gate = _sigmoid(_dot(x1.astype(BF16), gatew_ref[...]) + gate_b)
    ple = _dot(p_ref[0, 0].astype(BF16), plew_ref[...])
    o_ref[0] = x1 + ple * gate


def _pad_rows(rows, width):
    arr = jnp.stack(rows).astype(F32)
    return jnp.pad(arr, ((0, SUBLANES - arr.shape[0]), (0, 0)))


def _layer_call(i, x, p, consts, weights):
    bsz, seq, _ = x.shape
    ts = min(SEQ_TILE, seq)
    assert seq % ts == 0 and ts % SGU_BLOCK == 0 and ts % SCAN_CHUNK == 0 and ts % CONV_ROWS == 0
    seg, tri = consts

    def full(arr):
        nd = arr.ndim
        return pl.BlockSpec(arr.shape, lambda b, s, _nd=nd: (0,) * _nd)

    ins = [x, p] + list(weights[:9]) + [seg, tri] + list(weights[9:])
    in_specs = [pl.BlockSpec((1, ts, D_MODEL), lambda b, s: (b, s, 0)),
                pl.BlockSpec((1, 1, ts, PLE_DIM), lambda b, s, _i=i: (_i, b, s, 0))]
    in_specs += [full(a) for a in ins[2:]]
    scratch = [
        pltpu.VMEM((ts + CONV_HIST, B_WIDTH), F32),
        pltpu.VMEM((ts + SUBLANES, C_SHIFT_COLS), F32),
        pltpu.VMEM((C_WIDTH // LANES, LANES, LANES), F32),
    ]
    return pl.pallas_call(
        _layer_kernel,
        grid=(bsz, seq // ts),
        in_specs=in_specs,
        out_specs=pl.BlockSpec((1, ts, D_MODEL), lambda b, s: (b, s, 0)),
        out_shape=jax.ShapeDtypeStruct(x.shape, F32),
        scratch_shapes=scratch,
        compiler_params=pltpu.CompilerParams(
            dimension_semantics=("arbitrary", "arbitrary"),
            vmem_limit_bytes=VMEM_LIMIT_BYTES),
        name=f"layer{i}",
    )(*ins)


def kernel(x, p, pre_norm_g, w_in, sgu_ln_g, sgu_ln_b, sgu_w, sgu_b, conv_w, conv_b, conv_ln_g,
           conv_ln_b, pw_w, pw_b, shift_mu, w0, w_up, a0, a_up, k_k, k_a, r_k, lnx_g, lnx_b,
           w_out, post_norm_g, ple_w, ple_gate_w, ple_gate_b):
    depth = w_in.shape[0]
    seq = x.shape[1]
    ts = min(SEQ_TILE, seq)
    head = jnp.arange(SEG_WIDTH) // HEAD_DIM
    seg = (head[:, None] == head[None, :]).astype(BF16)
    pos = jnp.arange(ts)
    tri = ((pos[:, None] // SCAN_CHUNK == pos[None, :] // SCAN_CHUNK)
           & (pos[None, :] <= pos[:, None])).astype(BF16)
    zeros = jnp.zeros((LORA, C_WIDTH), F32)
    for i in range(depth):
        lora = jnp.concatenate([jnp.concatenate([w_up[i], zeros], axis=1),
                                jnp.concatenate([zeros, a_up[i]], axis=1)], axis=0).astype(BF16)
        weights = (
            w_in[i].astype(BF16), w_out[i].astype(BF16), ple_gate_w[i].astype(BF16),
            ple_w[i].astype(BF16),
            sgu_w[i].transpose(1, 0, 2).reshape(SGU_BLOCK, -1),
            jnp.repeat(sgu_b[i].T, HEAD_DIM, axis=1),
            jnp.pad(conv_w[i], ((0, 1), (0, 0))),
            pw_w[i].astype(BF16), lora,
            _pad_rows([pre_norm_g[i], post_norm_g[i], ple_gate_b[i]], D_MODEL),
            _pad_rows([sgu_ln_g[i], sgu_ln_b[i], conv_b[i], conv_ln_g[i], conv_ln_b[i], pw_b[i]],
                      B_WIDTH),
            _pad_rows([w0[i], a0[i], k_k[i], k_a[i], r_k[i].reshape(-1), lnx_g[i], lnx_b[i]],
                      C_WIDTH),
            shift_mu[i].reshape(1, -1),
        )
        x = _layer_call(i, x, p, (seg, tri), weights)
    return x
```
